```python
import math
import jax
import jax.numpy as jnp
from jax import lax
import numpy as np

D_MODEL = 4096
BATCH = 2
SEQ = 4096
DEPTH = 1
DEC_BATCH = 128
DEC_SEQ = 1
PAST_LEN = 2048
PAGE_SIZE = 128

GLA_HEADS = 8
GLA_DK = 128
GLA_DV = 256
GLA_QK_WIDTH = GLA_HEADS * GLA_DK
GLA_V_WIDTH = GLA_HEADS * GLA_DV
GLA_GATE_RANK = 16
GLA_GATE_TAU = 16.0
GLA_CHUNK = 64
MOBA_HEADS = 16
MOBA_HEAD_DIM = 128
MOBA_WIDTH = MOBA_HEADS * MOBA_HEAD_DIM
MOBA_BLOCK = 256
MOBA_TOPK = 3
MOBA_Q_CHUNK = 32
FFN_HIDDEN = -(-8 * D_MODEL // (3 * 256)) * 256
RMS_EPS = 1e-6
IN_SIZES = (GLA_QK_WIDTH, GLA_QK_WIDTH, GLA_V_WIDTH, GLA_V_WIDTH, GLA_GATE_RANK, MOBA_WIDTH, MOBA_WIDTH, MOBA_WIDTH)
D_IN = sum(IN_SIZES)

kernel_name = 'hybrid_gla_moba_decoder'


def rms_norm(x, w):
    x32 = x.astype(jnp.float32)
    y = x32 * lax.rsqrt(jnp.mean(x32 * x32, axis=-1, keepdims=True) + RMS_EPS)
    return (y * w.astype(jnp.float32)).astype(x.dtype)


def gla_chunked(q, k, v, log_a, state0):
    B, L, H, DK = q.shape
    DV = v.shape[-1]
    c = math.gcd(L, GLA_CHUNK)
    n = L // c

    def chunks(t):
        return t.astype(jnp.float32).reshape(B, n, c, H, t.shape[-1]).swapaxes(0, 1)

    causal = jnp.tril(jnp.ones((c, c), dtype=bool))[None, :, :, None, None]

    def step(S, inp):
        qc, kc, vc, ac = inp
        b = jnp.cumsum(ac, axis=1)
        o_inter = jnp.einsum('bthk,bhkv->bthv', qc * jnp.exp(b), S)
        decay = jnp.exp(jnp.where(causal, b[:, :, None] - b[:, None, :], -jnp.inf))
        A = jnp.einsum('bthk,bshk,btshk->bhts', qc, kc, decay)
        o_intra = jnp.einsum('bhts,bshv->bthv', A, vc)
        b_last = b[:, -1]
        S_new = S * jnp.exp(b_last)[..., None] + jnp.einsum('bshk,bshv->bhkv', kc * jnp.exp(b_last[:, None] - b), vc)
        return S_new, o_inter + o_intra

    S_fin, o = lax.scan(step, state0.astype(jnp.float32), (chunks(q), chunks(k), chunks(v), chunks(log_a)))
    return o.swapaxes(0, 1).reshape(B, L, H, DV), S_fin


def moba_attention(q, kb, vb, q_offset):
    B, Lq, H, hd = q.shape
    nb = kb.shape[1]
    n_gate = max(nb, MOBA_TOPK)
    k_mean = jnp.mean(kb, axis=2, dtype=jnp.float32)
    if n_gate > nb:
        k_mean = jnp.pad(k_mean, ((0, 0), (0, n_gate - nb), (0, 0), (0, 0)))
    qc = math.gcd(Lq, MOBA_Q_CHUNK)
    n = Lq // qc
    q_chunks = q.reshape(B, n, qc, H, hd).swapaxes(0, 1)
    pos_chunks = (q_offset + jnp.arange(Lq, dtype=jnp.int32)).reshape(n, qc)
    b_idx = jnp.arange(B)[:, None, None, None]
    h_idx = jnp.arange(H)[None, :, None, None]
    blk_off = jnp.arange(MOBA_BLOCK, dtype=jnp.int32)
    scale = hd ** -0.5

    def attend(args):
        qb, pos = args
        cur = pos // MOBA_BLOCK
        gate = jnp.einsum('bqhd,bnhd->bhqn', qb.astype(jnp.float32), k_mean)
        is_past = jnp.arange(n_gate)[None, :] < cur[:, None]
        gate = jnp.where(is_past, gate, -jnp.inf)
        _, top = lax.top_k(gate, MOBA_TOPK)
        own = jnp.broadcast_to(cur[None, None, :, None], (B, H, qc, 1))
        blocks = jnp.concatenate([jnp.minimum(top, nb - 1), own], axis=-1)
        valid = jnp.concatenate([top < cur[None, None, :, None], jnp.ones_like(own, dtype=bool)], axis=-1)
        kg = kb[b_idx, blocks, :, h_idx, :]
        vg = vb[b_idx, blocks, :, h_idx, :]
        s = jnp.einsum('bqhd,bhqjkd->bhqjk', qb, kg, preferred_element_type=jnp.float32) * scale
        kpos = blocks[..., None] * MOBA_BLOCK + blk_off
        mask = valid[..., None] & (kpos <= pos[None, None, :, None, None])
        p = jax.nn.softmax(jnp.where(mask, s, -jnp.inf).reshape(B, H, qc, -1), axis=-1)
        return jnp.einsum('bhqn,bhqnd->bhqd', p.astype(vb.dtype), vg.reshape(B, H, qc, -1, hd))

    o = lax.map(attend, (q_chunks, pos_chunks))
    return o.transpose(1, 0, 3, 2, 4).reshape(B, Lq, H * hd)


def hybrid_layer(x, k_past, v_past, gla_state0, params):
    (norm_mix_pre, norm_mix_post, w_in, w_gla_gate_up, b_gla_gate, gla_norm, w_merge,
     w_branch_gla, w_branch_moba, w_out, norm_ffn_pre, norm_ffn_post,
     w_ffn_gate, w_ffn_up, w_ffn_down) = params
    B, L, _ = x.shape
    P = k_past.shape[1]
    xn = rms_norm(x, norm_mix_pre)
    proj = xn @ w_in
    q_g, k_g, v_g, r_g, a_low, q_m, k_m, v_m = jnp.split(proj, np.cumsum(IN_SIZES)[:-1].tolist(), axis=-1)

    log_a = jax.nn.log_sigmoid((a_low @ w_gla_gate_up + b_gla_gate).astype(jnp.float32)) / GLA_GATE_TAU
    o_g, gla_state = gla_chunked(
        q_g.reshape(B, L, GLA_HEADS, GLA_DK) * (GLA_DK ** -0.5),
        k_g.reshape(B, L, GLA_HEADS, GLA_DK),
        v_g.reshape(B, L, GLA_HEADS, GLA_DV),
        log_a.reshape(B, L, GLA_HEADS, GLA_DK),
        gla_state0)
    o_g = rms_norm(o_g.astype(x.dtype), gla_norm).reshape(B, L, GLA_V_WIDTH) * jax.nn.silu(r_g)

    k_new = k_m.reshape(B, L, MOBA_HEADS, MOBA_HEAD_DIM)
    v_new = v_m.reshape(B, L, MOBA_HEADS, MOBA_HEAD_DIM)
    Lk = P + L
    nb = -(-Lk // MOBA_BLOCK)
    pad = jnp.zeros((B, nb * MOBA_BLOCK - Lk, MOBA_HEADS, MOBA_HEAD_DIM), k_new.dtype)
    kb = jnp.concatenate([k_past.astype(k_new.dtype), k_new, pad], axis=1).reshape(B, nb, MOBA_BLOCK, MOBA_HEADS, MOBA_HEAD_DIM)
    vb = jnp.concatenate([v_past.astype(v_new.dtype), v_new, pad], axis=1).reshape(B, nb, MOBA_BLOCK, MOBA_HEADS, MOBA_HEAD_DIM)
    o_m = moba_attention(q_m.reshape(B, L, MOBA_HEADS, MOBA_HEAD_DIM), kb, vb, P)

    gates = jax.nn.sigmoid((xn @ w_merge).astype(jnp.float32)).astype(x.dtype)
    g_a, g_b = jnp.split(gates, 2, axis=-1)
    mixed = g_a * (o_g @ w_branch_gla) + g_b * (o_m.astype(x.dtype) @ w_branch_moba)
    h = x + rms_norm(mixed @ w_out, norm_mix_post)

    hn = rms_norm(h, norm_ffn_pre)
    f = (jax.nn.silu(hn @ w_ffn_gate) * (hn @ w_ffn_up)) @ w_ffn_down
    y = h + rms_norm(f, norm_ffn_post)
    return y, k_new, v_new, gla_state


def setup_inputs(seed: int = 0) -> dict:
    key = jax.random.key(seed)
    ks = jax.random.split(key, 24)
    f32 = jnp.float32
    n_pages = PAST_LEN // PAGE_SIZE
    n_pool = (DEC_BATCH * n_pages * 5) // 4

    def normal(k, shape):
        return jax.random.normal(k, shape, f32)

    def dense(k, shape):
        return normal(k, shape) * (shape[0] ** -0.5)

    def gain(k, n):
        return 1.0 + 0.02 * normal(k, (n,))

    page_table = jax.random.permutation(ks[5], n_pool)[:DEC_BATCH * n_pages].reshape(DEC_BATCH, n_pages).astype(jnp.int32)
    return {
        'x_prompt': normal(ks[0], (BATCH, SEQ, D_MODEL)),
        'x_sample': normal(ks[1], (DEC_BATCH, DEC_SEQ, D_MODEL)),
        'cache_k': normal(ks[2], (n_pool, PAGE_SIZE, MOBA_HEADS, MOBA_HEAD_DIM)),
        'cache_v': normal(ks[3], (n_pool, PAGE_SIZE, MOBA_HEADS, MOBA_HEAD_DIM)),
        'state_gla': 0.5 * normal(ks[4], (DEC_BATCH, GLA_HEADS, GLA_DK, GLA_DV)),
        'page_table': page_table,
        'norm_mix_pre': gain(ks[6], D_MODEL),
        'norm_mix_post': gain(ks[7], D_MODEL),
        'w_in': dense(ks[8], (D_MODEL, D_IN)),
        'w_gla_gate_up': dense(ks[9], (GLA_GATE_RANK, GLA_QK_WIDTH)),
        'b_gla_gate': 0.1 * normal(ks[10], (GLA_QK_WIDTH,)),
        'gla_norm': gain(ks[11], GLA_DV),
        'w_merge': dense(ks[12], (D_MODEL, 2 * D_MODEL)),
        'w_branch_gla': dense(ks[13], (GLA_V_WIDTH, D_MODEL)),
        'w_branch_moba': dense(ks[14], (MOBA_WIDTH, D_MODEL)),
        'w_out': dense(ks[15], (D_MODEL, D_MODEL)),
        'norm_ffn_pre': gain(ks[16], D_MODEL),
        'norm_ffn_post': gain(ks[17], D_MODEL),
        'w_ffn_gate': dense(ks[18], (D_MODEL, FFN_HIDDEN)),
        'w_ffn_up': dense(ks[19], (D_MODEL, FFN_HIDDEN)),
        'w_ffn_down': dense(ks[20], (FFN_HIDDEN, D_MODEL)),
    }


def reference(x_prompt, x_sample, cache_k, cache_v, state_gla, page_table,
              norm_mix_pre, norm_mix_post, w_in, w_gla_gate_up, b_gla_gate, gla_norm, w_merge,
              w_branch_gla, w_branch_moba, w_out, norm_ffn_pre, norm_ffn_post,
              w_ffn_gate, w_ffn_up, w_ffn_down):
    params = (norm_mix_pre, norm_mix_post, w_in, w_gla_gate_up, b_gla_gate, gla_norm, w_merge,
              w_branch_gla, w_branch_moba, w_out, norm_ffn_pre, norm_ffn_post,
              w_ffn_gate, w_ffn_up, w_ffn_down)
    b_p = x_prompt.shape[0]
    empty_kv = jnp.zeros((b_p, 0, MOBA_HEADS, MOBA_HEAD_DIM), x_prompt.dtype)
    zero_state = jnp.zeros((b_p, GLA_HEADS, GLA_DK, GLA_DV), jnp.float32)
    db, n_pages = page_table.shape
    past = n_pages * cache_k.shape[1]
    k_past = cache_k[page_table].reshape(db, past, MOBA_HEADS, MOBA_HEAD_DIM)
    v_past = cache_v[page_table].reshape(db, past, MOBA_HEADS, MOBA_HEAD_DIM)
    y_prompt, y_sample = x_prompt, x_sample
    for _ in range(DEPTH):
        y_prompt, k_prompt, v_prompt, gla_state_prompt = hybrid_layer(y_prompt, empty_kv, empty_kv, zero_state, params)
        y_sample, k_sample, v_sample, gla_state_sample = hybrid_layer(y_sample, k_past, v_past, state_gla, params)
    return (y_prompt, y_sample, k_prompt, v_prompt, gla_state_prompt, k_sample, v_sample, gla_state_sample)
```

```python
import functools

import jax
import jax.numpy as jnp
from jax import lax
from jax.experimental import pallas as pl
from jax.experimental.pallas import tpu as pltpu

F32 = jnp.float32
BF16 = jnp.bfloat16

GLA_HEADS = 8
GLA_DK = 128
GLA_DV = 256
GLA_QK_WIDTH = GLA_HEADS * GLA_DK
GLA_V_WIDTH = GLA_HEADS * GLA_DV
GLA_GATE_RANK = 16
GLA_GATE_TAU = 16.0
GLA_CHUNK = 64
GLA_SUB = 16
MOBA_HEADS = 16
MOBA_HEAD_DIM = 128
MOBA_WIDTH = MOBA_HEADS * MOBA_HEAD_DIM
MOBA_BLOCK = 256
MOBA_TOPK = 3
RMS_EPS = 1e-6
LANE = 128
VMEM_LIMIT = 56 * 1024 * 1024
NEG_INF = float("-inf")


def _params(*sem):
    return pltpu.CompilerParams(dimension_semantics=sem, vmem_limit_bytes=VMEM_LIMIT)


def _sigmoid(x):
    return 1.0 / (1.0 + jnp.exp(-x))


def _log_sigmoid(x):
    return jnp.minimum(x, 0.0) - jnp.log1p(jnp.exp(-jnp.abs(x)))


def _dot(a, b):
    return jnp.dot(a, b, preferred_element_type=F32)


def _dot_nt(a, b, precision=None):
    return lax.dot_general(a, b, (((1,), (1,)), ((), ())), precision=precision, preferred_element_type=F32)


def _dot_tn(a, b):
    return lax.dot_general(a, b, (((0,), (0,)), ((), ())), preferred_element_type=F32)


def _rmsnorm_kernel(x_ref, w_ref, o_ref):
    x = x_ref[...]
    y = x * lax.rsqrt(jnp.mean(x * x, axis=-1, keepdims=True) + RMS_EPS)
    o_ref[...] = (y * w_ref[...]).astype(o_ref.dtype)


def rmsnorm(x, w, tm, out_dtype=BF16):
    m, d = x.shape
    return pl.pallas_call(
        _rmsnorm_kernel,
        out_shape=jax.ShapeDtypeStruct((m, d), out_dtype),
        grid=(m // tm,),
        in_specs=[pl.BlockSpec((tm, d), lambda i: (i, 0)), pl.BlockSpec((1, d), lambda i: (0, 0))],
        out_specs=pl.BlockSpec((tm, d), lambda i: (i, 0)),
        compiler_params=_params("parallel"),
        name="rmsnorm",
    )(x, w.reshape(1, d))


def _norm_residual_kernel(f_ref, x_ref, w_ref, h_ref, *hn_refs):
    f = f_ref[...]
    y = f * lax.rsqrt(jnp.mean(f * f, axis=-1, keepdims=True) + RMS_EPS)
    h = x_ref[...] + y * w_ref[...]
    h_ref[...] = h
    if hn_refs:
        wn_ref, hn_ref = hn_refs
        hn = h * lax.rsqrt(jnp.mean(h * h, axis=-1, keepdims=True) + RMS_EPS)
        hn_ref[...] = (hn * wn_ref[...]).astype(hn_ref.dtype)


def norm_residual(f, x, w, tm, w_next=None):
    m, d = x.shape
    row = pl.BlockSpec((tm, d), lambda i: (i, 0))
    vec = pl.BlockSpec((1, d), lambda i: (0, 0))
    if w_next is None:
        return pl.pallas_call(
            _norm_residual_kernel,
            out_shape=jax.ShapeDtypeStruct((m, d), F32),
            grid=(m // tm,), in_specs=[row, row, vec], out_specs=row,
            compiler_params=_params("parallel"), name="norm_residual",
        )(f, x, w.reshape(1, d))

    def kern(f_ref, x_ref, w_ref, wn_ref, h_ref, hn_ref):
        _norm_residual_kernel(f_ref, x_ref, w_ref, h_ref, wn_ref, hn_ref)

    return pl.pallas_call(
        kern,
        out_shape=(jax.ShapeDtypeStruct((m, d), F32), jax.ShapeDtypeStruct((m, d), BF16)),
        grid=(m // tm,), in_specs=[row, row, vec, vec], out_specs=(row, row),
        compiler_params=_params("parallel"), name="norm_residual_norm",
    )(f, x, w.reshape(1, d), w_next.reshape(1, d))


def _mm_kernel(x_ref, w_ref, o_ref):
    o_ref[...] = _dot(x_ref[...], w_ref[...]).astype(o_ref.dtype)


def matmul(x, w, tm, tn, out_dtype=F32):
    m, k = x.shape
    n = w.shape[1]
    return pl.pallas_call(
        _mm_kernel,
        out_shape=jax.ShapeDtypeStruct((m, n), out_dtype),
        grid=(m // tm, pl.cdiv(n, tn)),
        in_specs=[pl.BlockSpec((tm, k), lambda i, j: (i, 0)), pl.BlockSpec((k, tn), lambda i, j: (0, j))],
        out_specs=pl.BlockSpec((tm, tn), lambda i, j: (i, j)),
        compiler_params=_params("parallel", "arbitrary"),
        name="matmul",
    )(x, w)


def _mix_kernel(xn_ref, og_ref, om_ref, wga_ref, wgb_ref, wba_ref, wbb_ref, o_ref):
    xn = xn_ref[...]
    g_a = _sigmoid(_dot(xn, wga_ref[...]))
    g_b = _sigmoid(_dot(xn, wgb_ref[...]))
    mixed = g_a * _dot(og_ref[...], wba_ref[...]) + g_b * _dot(om_ref[...], wbb_ref[...])
    o_ref[...] = mixed.astype(o_ref.dtype)


def gated_mix(xn, o_g, o_m, w_merge, w_bg, w_bm, tm, tn):
    m, d = xn.shape
    nb = d // tn
    return pl.pallas_call(
        _mix_kernel,
        out_shape=jax.ShapeDtypeStruct((m, d), BF16),
        grid=(m // tm, nb),
        in_specs=[
            pl.BlockSpec((tm, d), lambda i, j: (i, 0)),
            pl.BlockSpec((tm, o_g.shape[1]), lambda i, j: (i, 0)),
            pl.BlockSpec((tm, o_m.shape[1]), lambda i, j: (i, 0)),
            pl.BlockSpec((d, tn), lambda i, j: (0, j)),
            pl.BlockSpec((d, tn), lambda i, j: (0, j + nb)),
            pl.BlockSpec((w_bg.shape[0], tn), lambda i, j: (0, j)),
            pl.BlockSpec((w_bm.shape[0], tn), lambda i, j: (0, j)),
        ],
        out_specs=pl.BlockSpec((tm, tn), lambda i, j: (i, j)),
        compiler_params=_params("parallel", "arbitrary"),
        name="gated_mix",
    )(xn, o_g, o_m, w_merge, w_merge, w_bg, w_bm)


def _swiglu_kernel(x_ref, wg_ref, wu_ref, o_ref):
    x = x_ref[...]
    g = _dot(x, wg_ref[...])
    u = _dot(x, wu_ref[...])
    o_ref[...] = (g * _sigmoid(g) * u).astype(o_ref.dtype)


def swiglu_up(x, w_gate, w_up, tm, tn):
    m, k = x.shape
    n = w_gate.shape[1]
    return pl.pallas_call(
        _swiglu_kernel,
        out_shape=jax.ShapeDtypeStruct((m, n), BF16),
        grid=(m // tm, pl.cdiv(n, tn)),
        in_specs=[pl.BlockSpec((tm, k), lambda i, j: (i, 0)),
                  pl.BlockSpec((k, tn), lambda i, j: (0, j)),
                  pl.BlockSpec((k, tn), lambda i, j: (0, j))],
        out_specs=pl.BlockSpec((tm, tn), lambda i, j: (i, j)),
        compiler_params=_params("parallel", "arbitrary"),
        name="swiglu_up",
    )(x, w_gate, w_up)


def _gla_head(q, k, v, b, s_t):
    c = q.shape[0]
    nsub = c // GLA_SUB
    row = lax.broadcasted_iota(jnp.int32, (c, c), 0)
    col = lax.broadcasted_iota(jnp.int32, (c, c), 1)
    diff = row - col
    row_blk = row // GLA_SUB
    same_sub = (row_blk == col // GLA_SUB) & (col <= row)

    o_inter = _dot_nt((q * jnp.exp(b)).astype(BF16), s_t.astype(BF16))

    a_band = jnp.zeros((c, c), F32)
    for d in range(GLA_SUB):
        k_s = k if d == 0 else pltpu.roll(k, d, 0)
        b_s = b if d == 0 else pltpu.roll(b, d, 0)
        w = q * k_s * jnp.exp(jnp.minimum(b - b_s, 0.0))
        a_band = jnp.where(diff == d, jnp.sum(w, axis=-1, keepdims=True), a_band)

    if nsub > 1:
        t_row = lax.broadcasted_iota(jnp.int32, (c, 1), 0)
        t_blk = t_row // GLA_SUB
        q_parts, k_parts = [], []
        for i in range(1, nsub):
            r_i = b[i * GLA_SUB - 1:i * GLA_SUB, :]
            q_i = jnp.where(t_blk == i, q * jnp.exp(jnp.minimum(b - r_i, 0.0)), 0.0)
            k_i = jnp.where(t_row < i * GLA_SUB, k * jnp.exp(jnp.minimum(r_i - b, 0.0)), 0.0)
            q_parts.append(q_i.astype(BF16))
            k_parts.append(k_i.astype(BF16))
        a_off = _dot_nt(jnp.concatenate(q_parts, axis=1), jnp.concatenate(k_parts, axis=1))
        a = jnp.where(same_sub, a_band, jnp.where(col < row_blk * GLA_SUB, a_off, 0.0))
    else:
        a = jnp.where(same_sub, a_band, 0.0)

    o = o_inter + _dot(a.astype(BF16), v.astype(BF16))
    b_last = b[c - 1:c, :]
    k_dec = k * jnp.exp(b_last - b)
    s_new = s_t * jnp.exp(b_last) + _dot_tn(v.astype(BF16), k_dec.astype(BF16))
    return o, s_new


def _gla_out(o, r, gnorm):
    y = o * lax.rsqrt(jnp.mean(o * o, axis=-1, keepdims=True) + RMS_EPS)
    return y * gnorm * (r * _sigmoid(r))


def _log_decay(alow, wup, bias):
    z = _dot(alow.astype(BF16), wup.astype(BF16)) + bias
    return _log_sigmoid(z) / GLA_GATE_TAU


def _gla_prompt_kernel(q_ref, k_ref, v_ref, r_ref, alow_ref, wup_ref, bias_ref, gnorm_ref,
                       o_ref, state_ref, st_ref):
    t = pl.program_id(1)
    c = q_ref.shape[0]

    @pl.when(t == 0)
    def _():
        st_ref[...] = jnp.zeros_like(st_ref)

    log_a = _log_decay(alow_ref[...], wup_ref[...], bias_ref[...])
    tril = (lax.broadcasted_iota(jnp.int32, (c, c), 1) <= lax.broadcasted_iota(jnp.int32, (c, c), 0)).astype(F32)
    b_all = jnp.dot(tril, log_a, precision=lax.Precision.HIGHEST, preferred_element_type=F32)
    gnorm = gnorm_ref[...]
    scale = GLA_DK ** -0.5
    for h in range(GLA_HEADS):
        ks = slice(h * GLA_DK, (h + 1) * GLA_DK)
        vs = slice(h * GLA_DV, (h + 1) * GLA_DV)
        o, s_new = _gla_head(q_ref[:, ks] * scale, k_ref[:, ks], v_ref[:, vs], b_all[:, ks], st_ref[h])
        st_ref[h] = s_new
        o_ref[:, vs] = _gla_out(o, r_ref[:, vs], gnorm).astype(o_ref.dtype)

    @pl.when(t == pl.num_programs(1) - 1)
    def _():
        for h in range(GLA_HEADS):
            state_ref[0, h] = st_ref[h].T


def gla_prompt(proj, alow, w_up, bias, gnorm, batch, seq):
    n_chunks = seq // GLA_CHUNK
    c = GLA_CHUNK

    def rows(b, t):
        return b * n_chunks + t

    return pl.pallas_call(
        _gla_prompt_kernel,
        out_shape=(jax.ShapeDtypeStruct((batch * seq, GLA_V_WIDTH), BF16),
                   jax.ShapeDtypeStruct((batch, GLA_HEADS, GLA_DK, GLA_DV), F32)),
        grid=(batch, n_chunks),
        in_specs=[
            pl.BlockSpec((c, GLA_QK_WIDTH), lambda b, t: (rows(b, t), 0)),
            pl.BlockSpec((c, GLA_QK_WIDTH), lambda b, t: (rows(b, t), 1)),
            pl.BlockSpec((c, GLA_V_WIDTH), lambda b, t: (rows(b, t), 1)),
            pl.BlockSpec((c, GLA_V_WIDTH), lambda b, t: (rows(b, t), 2)),
            pl.BlockSpec((c, LANE), lambda b, t: (rows(b, t), 0)),
            pl.BlockSpec((LANE, GLA_QK_WIDTH), lambda b, t: (0, 0)),
            pl.BlockSpec((1, GLA_QK_WIDTH), lambda b, t: (0, 0)),
            pl.BlockSpec((1, GLA_DV), lambda b, t: (0, 0)),
        ],
        out_specs=(pl.BlockSpec((c, GLA_V_WIDTH), lambda b, t: (rows(b, t), 0)),
                   pl.BlockSpec((1, GLA_HEADS, GLA_DK, GLA_DV), lambda b, t: (b, 0, 0, 0))),
        scratch_shapes=[pltpu.VMEM((GLA_HEADS, GLA_DV, GLA_DK), F32)],
        compiler_params=_params("parallel", "arbitrary"),
        name="gla_prompt",
    )(proj, proj, proj, proj, alow, w_up, bias, gnorm)


def _gla_step_kernel(qt_ref, kt_ref, at_ref, v_ref, r_ref, s_ref, gnorm_ref, o_ref, snew_ref):
    bb = s_ref.shape[0]
    scale = GLA_DK ** -0.5
    gnorm = gnorm_ref[...]
    for i in range(bb):
        q_t = qt_ref[i] * scale
        k_t = kt_ref[i]
        a_t = jnp.exp(at_ref[i])
        v = v_ref[i]
        r = r_ref[i]
        outs = []
        for h in range(GLA_HEADS):
            s_new = s_ref[i, h] * a_t[:, h:h + 1] + k_t[:, h:h + 1] * v[h:h + 1, :]
            snew_ref[i, h] = s_new
            outs.append(jnp.sum(q_t[:, h:h + 1] * s_new, axis=0, keepdims=True))
        o = jnp.concatenate(outs, axis=0)
        o_ref[i] = _gla_out(o, r, gnorm).astype(o_ref.dtype)


def gla_step(q_t, k_t, a_t, v, r, state, gnorm, bb):
    nb = state.shape[0]
    col = pl.BlockSpec((bb, GLA_DK, GLA_HEADS), lambda i: (i, 0, 0))
    val = pl.BlockSpec((bb, GLA_HEADS, GLA_DV), lambda i: (i, 0, 0))
    st = pl.BlockSpec((bb, GLA_HEADS, GLA_DK, GLA_DV), lambda i: (i, 0, 0, 0))
    return pl.pallas_call(
        _gla_step_kernel,
        out_shape=(jax.ShapeDtypeStruct((nb, GLA_HEADS, GLA_DV), BF16),
                   jax.ShapeDtypeStruct(state.shape, F32)),
        grid=(nb // bb,),
        in_specs=[col, col, col, val, val, st, pl.BlockSpec((1, GLA_DV), lambda i: (0, 0))],
        out_specs=(val, st),
        compiler_params=_params("parallel"),
        name="gla_step",
    )(q_t, k_t, a_t, v, r, state, gnorm)


def _log_decay_kernel(alow_ref, wup_ref, bias_ref, o_ref):
    o_ref[...] = _log_decay(alow_ref[...], wup_ref[...], bias_ref[...])


def log_decay(alow, w_up, bias):
    m = alow.shape[0]
    return pl.pallas_call(
        _log_decay_kernel,
        out_shape=jax.ShapeDtypeStruct((m, GLA_QK_WIDTH), F32),
        name="log_decay",
    )(alow, w_up, bias)


def _topk_select(gate, n_valid, col_idx, lane):
    g_n = jnp.sum(jnp.where(lane == col_idx, gate, 0.0), axis=-1, keepdims=True)
    beats = ((gate > g_n) | ((gate == g_n) & (lane < col_idx))) & (lane < n_valid)
    rank = jnp.sum(beats.astype(F32), axis=-1, keepdims=True)
    return rank < MOBA_TOPK


def _moba_prompt_kernel(q_ref, k_ref, v_ref, o_ref, kb_ref, vb_ref, kmean_ref):
    c = pl.program_id(2)
    blk = MOBA_BLOCK
    nb = k_ref.shape[0] // blk
    scale = MOBA_HEAD_DIM ** -0.5

    @pl.when(c == 0)
    def _():
        for n in range(nb):
            kn = k_ref[n * blk:(n + 1) * blk, :]
            kmean_ref[n:n + 1, :] = jnp.sum(kn, axis=0, keepdims=True) * (1.0 / blk)
            kb_ref[n * blk:(n + 1) * blk, :] = kn.astype(BF16)
            vb_ref[n * blk:(n + 1) * blk, :] = v_ref[n * blk:(n + 1) * blk, :].astype(BF16)

    q = q_ref[...]
    qb = q.astype(BF16)
    gate = _dot_nt(q, kmean_ref[...], precision=lax.Precision.HIGHEST)
    lane = lax.broadcasted_iota(jnp.int32, gate.shape, 1)

    own = pl.multiple_of(c * blk, blk)
    s = _dot_nt(qb, kb_ref[pl.ds(own, blk), :]) * scale
    causal = lax.broadcasted_iota(jnp.int32, s.shape, 1) <= lax.broadcasted_iota(jnp.int32, s.shape, 0)
    s = jnp.where(causal, s, NEG_INF)
    m0 = jnp.max(s, axis=-1, keepdims=True)
    p = jnp.exp(s - m0)
    l0 = jnp.sum(p, axis=-1, keepdims=True)
    acc0 = _dot(p.astype(BF16), vb_ref[pl.ds(own, blk), :])

    def body(n, carry):
        m, l, acc = carry
        sel = _topk_select(gate, c, n, lane)
        start = pl.multiple_of(n * blk, blk)
        s = _dot_nt(qb, kb_ref[pl.ds(start, blk), :]) * scale
        s = jnp.where(sel, s, NEG_INF)
        m_new = jnp.maximum(m, jnp.max(s, axis=-1, keepdims=True))
        alpha = jnp.exp(m - m_new)
        p = jnp.exp(s - m_new)
        l = alpha * l + jnp.sum(p, axis=-1, keepdims=True)
        acc = alpha * acc + _dot(p.astype(BF16), vb_ref[pl.ds(start, blk), :])
        return m_new, l, acc

    m, l, acc = lax.fori_loop(0, c, body, (m0, l0, acc0))
    o_ref[...] = (acc / l).astype(o_ref.dtype)


def moba_prompt(proj, batch, seq, col0):
    nqb = seq // MOBA_BLOCK
    h_n = MOBA_HEADS
    return pl.pallas_call(
        _moba_prompt_kernel,
        out_shape=jax.ShapeDtypeStruct((batch * seq, MOBA_WIDTH), BF16),
        grid=(batch, h_n, nqb),
        in_specs=[
            pl.BlockSpec((MOBA_BLOCK, MOBA_HEAD_DIM), lambda b, h, c: (b * nqb + c, col0 + h)),
            pl.BlockSpec((seq, MOBA_HEAD_DIM), lambda b, h, c: (b, col0 + h_n + h)),
            pl.BlockSpec((seq, MOBA_HEAD_DIM), lambda b, h, c: (b, col0 + 2 * h_n + h)),
        ],
        out_specs=pl.BlockSpec((MOBA_BLOCK, MOBA_HEAD_DIM), lambda b, h, c: (b * nqb + c, h)),
        scratch_shapes=[pltpu.VMEM((seq, MOBA_HEAD_DIM), BF16), pltpu.VMEM((seq, MOBA_HEAD_DIM), BF16),
                        pltpu.VMEM((seq // MOBA_BLOCK, MOBA_HEAD_DIM), F32)],
        compiler_params=_params("parallel", "parallel", "arbitrary"),
        name="moba_prompt",
    )(proj, proj, proj)


def _moba_step_kernel(pt_ref, q_ref, kn_ref, vn_ref, k0_ref, k1_ref, v0_ref, v1_ref, o_ref,
                      m_ref, l_ref, g_ref, acc_ref):
    n = pl.program_id(1)
    n_blocks = pl.num_programs(1)
    hn, hd = MOBA_HEADS, MOBA_HEAD_DIM
    page = k0_ref.shape[1]
    scale = hd ** -0.5
    q = q_ref[0]
    qb = q.astype(BF16)
    lanes = page * hn
    same_head = (lax.broadcasted_iota(jnp.int32, (hn, lanes), 1) % hn) == lax.broadcasted_iota(jnp.int32, (hn, lanes), 0)

    s_parts = []
    for k_ref in (k0_ref, k1_ref):
        k2 = k_ref[0].reshape(lanes, hd).astype(BF16)
        s_parts.append(_dot_nt(qb, k2))
    s_raw = [jnp.where(same_head, s, 0.0) for s in s_parts]
    g_ref[n] = (jnp.sum(s_raw[0], axis=-1, keepdims=True) + jnp.sum(s_raw[1], axis=-1, keepdims=True))
    s_m = [jnp.where(same_head, s * scale, NEG_INF) for s in s_parts]
    m = jnp.maximum(jnp.max(s_m[0], axis=-1, keepdims=True), jnp.max(s_m[1], axis=-1, keepdims=True))
    p = [jnp.exp(s - m) for s in s_m]
    m_ref[n] = m
    l_ref[n] = jnp.sum(p[0], axis=-1, keepdims=True) + jnp.sum(p[1], axis=-1, keepdims=True)
    acc = jnp.zeros((hn, hd), F32)
    for p_i, v_ref in zip(p, (v0_ref, v1_ref)):
        acc += _dot(p_i.astype(BF16), v_ref[0].reshape(lanes, hd).astype(BF16))
    acc_ref[n] = acc

    @pl.when(n == n_blocks - 1)
    def _():
        nbk = m_ref.shape[0]
        gates = [g_ref[i] for i in range(nbk)]
        s_own = jnp.sum(q * kn_ref[0], axis=-1, keepdims=True) * scale
        sels = []
        m_all = s_own
        for i in range(nbk):
            rank = jnp.zeros_like(s_own)
            for j in range(nbk):
                if j == i:
                    continue
                beats = (gates[j] > gates[i]) | ((gates[j] == gates[i]) & (j < i))
                rank += beats.astype(F32)
            sel = rank < MOBA_TOPK
            sels.append(sel)
            m_all = jnp.where(sel, jnp.maximum(m_all, m_ref[i]), m_all)
        w_own = jnp.exp(s_own - m_all)
        den = w_own
        num = w_own * vn_ref[0]
        for i in range(nbk):
            w = jnp.where(sels[i], jnp.exp(m_ref[i] - m_all), 0.0)
            den += w * l_ref[i]
            num += w * acc_ref[i]
        o_ref[0] = (num / den).astype(o_ref.dtype)


def moba_step(q, k_new, v_new, cache_k, cache_v, page_table):
    nb, n_pages = page_table.shape
    page = cache_k.shape[1]
    pages_per_block = MOBA_BLOCK // page
    assert pages_per_block == 2
    n_blocks = n_pages // pages_per_block
    hn, hd = MOBA_HEADS, MOBA_HEAD_DIM
    tok = pl.BlockSpec((1, hn, hd), lambda b, n, pt: (b, 0, 0))

    def page_spec(j):
        return pl.BlockSpec((1, page, hn, hd), lambda b, n, pt: (pt[b * n_pages + n * pages_per_block + j], 0, 0, 0))

    return pl.pallas_call(
        _moba_step_kernel,
        out_shape=jax.ShapeDtypeStruct((nb, hn, hd), BF16),
        grid_spec=pltpu.PrefetchScalarGridSpec(
            num_scalar_prefetch=1,
            grid=(nb, n_blocks),
            in_specs=[tok, tok, tok, page_spec(0), page_spec(1), page_spec(0), page_spec(1)],
            out_specs=tok,
            scratch_shapes=[pltpu.VMEM((n_blocks, hn, 1), F32), pltpu.VMEM((n_blocks, hn, 1), F32),
                            pltpu.VMEM((n_blocks, hn, 1), F32), pltpu.VMEM((n_blocks, hn, hd), F32)],
        ),
        compiler_params=_params("parallel", "arbitrary"),
        name="moba_step",
    )(page_table.reshape(-1), q, k_new, v_new, cache_k, cache_k, cache_v, cache_v)


def _dense_tail(x, xn, o_g, o_m, wts, tm):
    (w_merge, w_bg, w_bm, w_out, w_gate, w_up, w_down, norm_mix_post, norm_ffn_pre, norm_ffn_post) = wts
    mixed = gated_mix(xn, o_g, o_m, w_merge, w_bg, w_bm, tm, 512)
    h, hn = norm_residual(matmul(mixed, w_out, tm, 512), x, norm_mix_post, min(tm, 256), norm_ffn_pre)
    a = swiglu_up(hn, w_gate, w_up, tm, 256)
    return norm_residual(matmul(a, w_down, tm, 512), h, norm_ffn_post, min(tm, 256))


def kernel(x_prompt, x_sample, cache_k, cache_v, state_gla, page_table, norm_mix_pre, norm_mix_post, w_in,
           w_gla_gate_up, b_gla_gate, gla_norm, w_merge, w_branch_gla, w_branch_moba, w_out, norm_ffn_pre,
           norm_ffn_post, w_ffn_gate, w_ffn_up, w_ffn_down):
    batch, seq, d = x_prompt.shape
    nb_s = x_sample.shape[0]
    gla_w = 2 * GLA_QK_WIDTH + 2 * GLA_V_WIDTH
    moba0 = gla_w + GLA_GATE_RANK

    w_in_gla = w_in[:, :gla_w].astype(BF16)
    w_in_low = jnp.pad(w_in[:, gla_w:moba0], ((0, 0), (0, LANE - GLA_GATE_RANK))).astype(BF16)
    w_in_moba = w_in[:, moba0:].astype(BF16)
    w_up_pad = jnp.pad(w_gla_gate_up, ((0, LANE - GLA_GATE_RANK), (0, 0)))
    bias = b_gla_gate.reshape(1, -1)
    gnorm = gla_norm.reshape(1, -1)
    tail_w = (w_merge.astype(BF16), w_branch_gla.astype(BF16), w_branch_moba.astype(BF16), w_out.astype(BF16),
              w_ffn_gate.astype(BF16), w_ffn_up.astype(BF16), w_ffn_down.astype(BF16),
              norm_mix_post, norm_ffn_pre, norm_ffn_post)

    xp = x_prompt.reshape(batch * seq, d)
    tm = 512
    xn = rmsnorm(xp, norm_mix_pre, 256)
    p_gla = matmul(xn, w_in_gla, tm, 512)
    p_low = matmul(xn, w_in_low, tm, LANE)
    p_moba = matmul(xn, w_in_moba, tm, 512)
    o_g, state_p = gla_prompt(p_gla, p_low, w_up_pad, bias, gnorm, batch, seq)
    o_m = moba_prompt(p_moba, batch, seq, 0)
    y_p = _dense_tail(xp, xn, o_g, o_m, tail_w, tm)
    k_p = p_moba[:, MOBA_WIDTH:2 * MOBA_WIDTH].reshape(batch, seq, MOBA_HEADS, MOBA_HEAD_DIM)
    v_p = p_moba[:, 2 * MOBA_WIDTH:].reshape(batch, seq, MOBA_HEADS, MOBA_HEAD_DIM)

    xs = x_sample.reshape(nb_s, d)
    tms = nb_s
    xns = rmsnorm(xs, norm_mix_pre, tms)
    s_gla = matmul(xns, w_in_gla, tms, 512)
    s_low = matmul(xns, w_in_low, tms, LANE)
    s_moba = matmul(xns, w_in_moba, tms, 512)
    log_a = log_decay(s_low, w_up_pad, bias)

    def cols(t):
        return t.reshape(nb_s, GLA_HEADS, GLA_DK).transpose(0, 2, 1)

    o_gs, state_s = gla_step(
        cols(s_gla[:, :GLA_QK_WIDTH]), cols(s_gla[:, GLA_QK_WIDTH:2 * GLA_QK_WIDTH]), cols(log_a),
        s_gla[:, 2 * GLA_QK_WIDTH:2 * GLA_QK_WIDTH + GLA_V_WIDTH].reshape(nb_s, GLA_HEADS, GLA_DV),
        s_gla[:, 2 * GLA_QK_WIDTH + GLA_V_WIDTH:].reshape(nb_s, GLA_HEADS, GLA_DV),
        state_gla, gnorm, 4)
    q_s = s_moba[:, :MOBA_WIDTH].reshape(nb_s, MOBA_HEADS, MOBA_HEAD_DIM)
    k_s = s_moba[:, MOBA_WIDTH:2 * MOBA_WIDTH].reshape(nb_s, MOBA_HEADS, MOBA_HEAD_DIM)
    v_s = s_moba[:, 2 * MOBA_WIDTH:].reshape(nb_s, MOBA_HEADS, MOBA_HEAD_DIM)
    o_ms = moba_step(q_s, k_s, v_s, cache_k, cache_v, page_table)
    y_s = _dense_tail(xs, xns, o_gs.reshape(nb_s, GLA_V_WIDTH), o_ms.reshape(nb_s, MOBA_WIDTH), tail_w, tms)

    return (y_p.reshape(batch, seq, d), y_s.reshape(nb_s, 1, d), k_p, v_p, state_p,
            k_s.reshape(nb_s, 1, MOBA_HEADS, MOBA_HEAD_DIM), v_s.reshape(nb_s, 1, MOBA_HEADS, MOBA_HEAD_DIM),
            state_s)
```

```python
import math

import jax
import jax.numpy as jnp
from jax import lax
from jax.experimental import pallas as pl
from jax.experimental.pallas import tpu as pltpu

F32 = jnp.float32
BF16 = jnp.bfloat16

GLA_HEADS = 8
GLA_DK = 128
GLA_DV = 256
GLA_QK_WIDTH = GLA_HEADS * GLA_DK
GLA_V_WIDTH = GLA_HEADS * GLA_DV
GLA_GATE_RANK = 16
GLA_GATE_TAU = 16.0
GLA_CHUNK = 64
GLA_SUB = 16
MOBA_HEADS = 16
MOBA_HEAD_DIM = 128
MOBA_WIDTH = MOBA_HEADS * MOBA_HEAD_DIM
MOBA_BLOCK = 256
MOBA_TOPK = 3
RMS_EPS = 1e-6
LANE = 128
VMEM_LIMIT = 56 * 1024 * 1024
NEG_INF = float("-inf")
LOG2_E = math.log2(math.e)
ROW_TILE = 128


def _params(*sem):
    return pltpu.CompilerParams(dimension_semantics=sem, vmem_limit_bytes=VMEM_LIMIT)


def _sigmoid(x):
    return 1.0 / (1.0 + jnp.exp(-x))


def _log_sigmoid(x):
    return jnp.minimum(x, 0.0) - jnp.log1p(jnp.exp(-jnp.abs(x)))


def _dot(a, b, precision=None):
    return jnp.dot(a, b, precision=precision, preferred_element_type=F32)


def _dot_nt(a, b):
    return lax.dot_general(a, b, (((1,), (1,)), ((), ())), preferred_element_type=F32)


def _dot_tn(a, b):
    return lax.dot_general(a, b, (((0,), (0,)), ((), ())), preferred_element_type=F32)


def _bf16(w):
    return w if w.dtype == BF16 else w.astype(BF16)


def _rms(x, w):
    return x * lax.rsqrt(jnp.mean(x * x, axis=-1, keepdims=True) + RMS_EPS) * w


def _group_specs(mp, ms, d):
    npb = mp // ROW_TILE
    assert mp % ROW_TILE == 0 and ms % ROW_TILE == 0
    p_spec = pl.BlockSpec((ROW_TILE, d), lambda i: (jnp.minimum(i, npb - 1), 0))
    s_spec = pl.BlockSpec((ROW_TILE, d), lambda i: (jnp.maximum(i - npb, 0), 0))
    return npb, p_spec, s_spec


def rmsnorm_groups(x_p, x_s, w):
    (mp, d), ms = x_p.shape, x_s.shape[0]
    npb, p_spec, s_spec = _group_specs(mp, ms, d)

    def kern(xp_ref, xs_ref, w_ref, o_ref):
        i = pl.program_id(0)

        @pl.when(i < npb)
        def _():
            o_ref[...] = _rms(xp_ref[...], w_ref[...]).astype(o_ref.dtype)

        @pl.when(i >= npb)
        def _():
            o_ref[...] = _rms(xs_ref[...], w_ref[...]).astype(o_ref.dtype)

    return pl.pallas_call(
        kern,
        out_shape=jax.ShapeDtypeStruct((mp + ms, d), BF16),
        grid=((mp + ms) // ROW_TILE,),
        in_specs=[p_spec, s_spec, pl.BlockSpec((1, d), lambda i: (0, 0))],
        out_specs=pl.BlockSpec((ROW_TILE, d), lambda i: (i, 0)),
        compiler_params=_params("arbitrary"),
        name="rmsnorm_groups",
    )(x_p, x_s, w.reshape(1, d))


def norm_residual_groups(f, x_p, x_s, w, w_next):
    (mp, d), ms = x_p.shape, x_s.shape[0]
    npb, p_spec, s_spec = _group_specs(mp, ms, d)
    row = pl.BlockSpec((ROW_TILE, d), lambda i: (i, 0))
    vec = pl.BlockSpec((1, d), lambda i: (0, 0))

    def kern(f_ref, xp_ref, xs_ref, w_ref, wn_ref, h_ref, hn_ref):
        i = pl.program_id(0)

        def body(x_ref):
            h = x_ref[...] + _rms(f_ref[...], w_ref[...])
            h_ref[...] = h
            hn_ref[...] = _rms(h, wn_ref[...]).astype(hn_ref.dtype)

        pl.when(i < npb)(lambda: body(xp_ref))
        pl.when(i >= npb)(lambda: body(xs_ref))

    return pl.pallas_call(
        kern,
        out_shape=(jax.ShapeDtypeStruct((mp + ms, d), F32), jax.ShapeDtypeStruct((mp + ms, d), BF16)),
        grid=((mp + ms) // ROW_TILE,),
        in_specs=[row, p_spec, s_spec, vec, vec],
        out_specs=(row, row),
        compiler_params=_params("arbitrary"),
        name="norm_residual_groups",
    )(f, x_p, x_s, w.reshape(1, d), w_next.reshape(1, d))


def norm_residual_rows(f, h, w, row0, rows):
    d = h.shape[1]
    b0 = row0 // ROW_TILE
    assert row0 % ROW_TILE == 0 and rows % ROW_TILE == 0

    def kern(f_ref, h_ref, w_ref, o_ref):
        o_ref[...] = h_ref[...] + _rms(f_ref[...], w_ref[...])

    src = pl.BlockSpec((ROW_TILE, d), lambda i: (i + b0, 0))
    return pl.pallas_call(
        kern,
        out_shape=jax.ShapeDtypeStruct((rows, d), F32),
        grid=(rows // ROW_TILE,),
        in_specs=[src, src, pl.BlockSpec((1, d), lambda i: (0, 0))],
        out_specs=pl.BlockSpec((ROW_TILE, d), lambda i: (i, 0)),
        compiler_params=_params("arbitrary"),
        name="norm_residual_rows",
    )(f, h, w.reshape(1, d))


def _mm_kernel(x_ref, w_ref, o_ref):
    o_ref[...] = _dot(x_ref[...], _bf16(w_ref[...])).astype(o_ref.dtype)


def matmul(x, w, tm, tn, n_cols=None, col0=0, out_dtype=F32):
    m, k = x.shape
    n = w.shape[1] if n_cols is None else n_cols
    assert m % tm == 0
    return pl.pallas_call(
        _mm_kernel,
        out_shape=jax.ShapeDtypeStruct((m, n), out_dtype),
        grid=(m // tm, pl.cdiv(n, tn)),
        in_specs=[pl.BlockSpec((tm, k), lambda i, j: (i, 0)), pl.BlockSpec((k, tn), lambda i, j: (0, j + col0))],
        out_specs=pl.BlockSpec((tm, tn), lambda i, j: (i, j)),
        compiler_params=_params("parallel", "arbitrary"),
        name="matmul",
    )(x, w)


def _mix_kernel(xn_ref, og_ref, om_ref, wga_ref, wgb_ref, wba_ref, wbb_ref, o_ref):
    xn = xn_ref[...]
    g_a = _sigmoid(_dot(xn, _bf16(wga_ref[...])))
    g_b = _sigmoid(_dot(xn, _bf16(wgb_ref[...])))
    mixed = g_a * _dot(og_ref[...], _bf16(wba_ref[...])) + g_b * _dot(om_ref[...], _bf16(wbb_ref[...]))
    o_ref[...] = mixed.astype(o_ref.dtype)


def gated_mix(xn, o_g, o_m, w_merge, w_bg, w_bm, tm, tn):
    m, d = xn.shape
    nb = d // tn
    assert m % tm == 0 and d % tn == 0

    def resident(width):
        return pl.BlockSpec((tm, width), lambda i, j: (i, 0), pipeline_mode=pl.Buffered(1))

    return pl.pallas_call(
        _mix_kernel,
        out_shape=jax.ShapeDtypeStruct((m, d), BF16),
        grid=(m // tm, nb),
        in_specs=[
            resident(d), resident(o_g.shape[1]), resident(o_m.shape[1]),
            pl.BlockSpec((d, tn), lambda i, j: (0, j)),
            pl.BlockSpec((d, tn), lambda i, j: (0, j + nb)),
            pl.BlockSpec((w_bg.shape[0], tn), lambda i, j: (0, j)),
            pl.BlockSpec((w_bm.shape[0], tn), lambda i, j: (0, j)),
        ],
        out_specs=pl.BlockSpec((tm, tn), lambda i, j: (i, j)),
        compiler_params=_params("parallel", "arbitrary"),
        name="gated_mix",
    )(xn, o_g, o_m, w_merge, w_merge, w_bg, w_bm)


def _swiglu_kernel(x_ref, wg_ref, wu_ref, o_ref):
    x = x_ref[...]
    g = _dot(x, _bf16(wg_ref[...]))
    u = _dot(x, _bf16(wu_ref[...]))
    o_ref[...] = (g * _sigmoid(g) * u).astype(o_ref.dtype)


def swiglu_up(x, w_gate, w_up, tm, tn):
    m, k = x.shape
    n = w_gate.shape[1]
    assert m % tm == 0 and n % tn == 0
    return pl.pallas_call(
        _swiglu_kernel,
        out_shape=jax.ShapeDtypeStruct((m, n), BF16),
        grid=(m // tm, n // tn),
        in_specs=[pl.BlockSpec((tm, k), lambda i, j: (i, 0)),
                  pl.BlockSpec((k, tn), lambda i, j: (0, j)),
                  pl.BlockSpec((k, tn), lambda i, j: (0, j))],
        out_specs=pl.BlockSpec((tm, tn), lambda i, j: (i, j)),
        compiler_params=_params("parallel", "arbitrary"),
        name="swiglu_up",
    )(x, w_gate, w_up)


def _gla_head(q, k, v, b, s_t):
    c = q.shape[0]
    nsub = c // GLA_SUB
    row = lax.broadcasted_iota(jnp.int32, (c, c), 0)
    col = lax.broadcasted_iota(jnp.int32, (c, c), 1)
    diff = row - col
    row_blk = row // GLA_SUB
    same_sub = (row_blk == col // GLA_SUB) & (col <= row)

    o_inter = _dot_nt((q * jnp.exp(b)).astype(BF16), s_t.astype(BF16))

    a_band = jnp.zeros((c, c), F32)
    for d in range(GLA_SUB):
        k_s = k if d == 0 else pltpu.roll(k, d, 0)
        b_s = b if d == 0 else pltpu.roll(b, d, 0)
        w = q * k_s * jnp.exp(jnp.minimum(b - b_s, 0.0))
        a_band = jnp.where(diff == d, jnp.sum(w, axis=-1, keepdims=True), a_band)

    if nsub > 1:
        t_row = lax.broadcasted_iota(jnp.int32, (c, 1), 0)
        t_blk = t_row // GLA_SUB
        q_parts, k_parts = [], []
        for i in range(1, nsub):
            r_i = b[i * GLA_SUB - 1:i * GLA_SUB, :]
            q_i = jnp.where(t_blk == i, q * jnp.exp(jnp.minimum(b - r_i, 0.0)), 0.0)
            k_i = jnp.where(t_row < i * GLA_SUB, k * jnp.exp(jnp.minimum(r_i - b, 0.0)), 0.0)
            q_parts.append(q_i.astype(BF16))
            k_parts.append(k_i.astype(BF16))
        a_off = _dot_nt(jnp.concatenate(q_parts, axis=1), jnp.concatenate(k_parts, axis=1))
        a = jnp.where(same_sub, a_band, jnp.where(col < row_blk * GLA_SUB, a_off, 0.0))
    else:
        a = jnp.where(same_sub, a_band, 0.0)

    o = o_inter + _dot(a.astype(BF16), v.astype(BF16))
    b_last = b[c - 1:c, :]
    k_dec = k * jnp.exp(b_last - b)
    s_new = s_t * jnp.exp(b_last) + _dot_tn(v.astype(BF16), k_dec.astype(BF16))
    return o, s_new


def _gla_out(o, r, gnorm):
    return _rms(o, gnorm) * (r * _sigmoid(r))


def _log_decay(alow, wup, bias):
    z = _dot(alow.astype(BF16), wup.astype(BF16)) + bias
    return _log_sigmoid(z) / GLA_GATE_TAU


def _gla_prompt_kernel(q_ref, k_ref, v_ref, r_ref, alow_ref, wup_ref, bias_ref, gnorm_ref,
                       o_ref, state_ref, st_ref):
    t = pl.program_id(1)
    c = q_ref.shape[0]

    @pl.when(t == 0)
    def _():
        st_ref[...] = jnp.zeros_like(st_ref)

    log_a = _log_decay(alow_ref[...], wup_ref[...], bias_ref[...])
    tril = (lax.broadcasted_iota(jnp.int32, (c, c), 1) <= lax.broadcasted_iota(jnp.int32, (c, c), 0)).astype(F32)
    b_all = _dot(tril, log_a, precision=lax.Precision.HIGHEST)
    gnorm = gnorm_ref[...]
    scale = GLA_DK ** -0.5
    for h in range(GLA_HEADS):
        ks = slice(h * GLA_DK, (h + 1) * GLA_DK)
        vs = slice(h * GLA_DV, (h + 1) * GLA_DV)
        o, s_new = _gla_head(q_ref[:, ks] * scale, k_ref[:, ks], v_ref[:, vs], b_all[:, ks], st_ref[h])
        st_ref[h] = s_new
        o_ref[:, vs] = _gla_out(o, r_ref[:, vs], gnorm).astype(o_ref.dtype)

    @pl.when(t == pl.num_programs(1) - 1)
    def _():
        for h in range(GLA_HEADS):
            state_ref[0, h] = st_ref[h].T


def gla_prompt(proj, alow, w_up, bias, gnorm, batch, seq, out_rows):
    n_chunks = seq // GLA_CHUNK
    c = GLA_CHUNK

    def rows(b, t):
        return b * n_chunks + t

    return pl.pallas_call(
        _gla_prompt_kernel,
        out_shape=(jax.ShapeDtypeStruct((out_rows, GLA_V_WIDTH), BF16),
                   jax.ShapeDtypeStruct((batch, GLA_HEADS, GLA_DK, GLA_DV), F32)),
        grid=(batch, n_chunks),
        in_specs=[
            pl.BlockSpec((c, GLA_QK_WIDTH), lambda b, t: (rows(b, t), 0)),
            pl.BlockSpec((c, GLA_QK_WIDTH), lambda b, t: (rows(b, t), 1)),
            pl.BlockSpec((c, GLA_V_WIDTH), lambda b, t: (rows(b, t), 1)),
            pl.BlockSpec((c, GLA_V_WIDTH), lambda b, t: (rows(b, t), 2)),
            pl.BlockSpec((c, LANE), lambda b, t: (rows(b, t), 0)),
            pl.BlockSpec((LANE, GLA_QK_WIDTH), lambda b, t: (0, 0)),
            pl.BlockSpec((1, GLA_QK_WIDTH), lambda b, t: (0, 0)),
            pl.BlockSpec((1, GLA_DV), lambda b, t: (0, 0)),
        ],
        out_specs=(pl.BlockSpec((c, GLA_V_WIDTH), lambda b, t: (rows(b, t), 0)),
                   pl.BlockSpec((1, GLA_HEADS, GLA_DK, GLA_DV), lambda b, t: (b, 0, 0, 0))),
        scratch_shapes=[pltpu.VMEM((GLA_HEADS, GLA_DV, GLA_DK), F32)],
        compiler_params=_params("parallel", "arbitrary"),
        name="gla_prompt",
    )(proj, proj, proj, proj, alow, w_up, bias, gnorm)


def _gla_step_kernel(qt_ref, kt_ref, at_ref, v_ref, r_ref, s_ref, gnorm_ref, o_ref, snew_ref):
    bb = s_ref.shape[0]
    scale = GLA_DK ** -0.5
    gnorm = gnorm_ref[...]
    for i in range(bb):
        q_t = qt_ref[i] * scale
        k_t = kt_ref[i]
        a_t = jnp.exp(at_ref[i])
        v = v_ref[i]
        r = r_ref[i]
        outs = []
        for h in range(GLA_HEADS):
            s_new = s_ref[i, h] * a_t[:, h:h + 1] + k_t[:, h:h + 1] * v[h:h + 1, :]
            snew_ref[i, h] = s_new
            outs.append(jnp.sum(q_t[:, h:h + 1] * s_new, axis=0, keepdims=True))
        o = jnp.concatenate(outs, axis=0)
        o_ref[i] = _gla_out(o, r, gnorm).astype(o_ref.dtype)


def gla_step(q_t, k_t, a_t, v, r, state, gnorm, bb):
    nb = state.shape[0]
    col = pl.BlockSpec((bb, GLA_DK, GLA_HEADS), lambda i: (i, 0, 0))
    val = pl.BlockSpec((bb, GLA_HEADS, GLA_DV), lambda i: (i, 0, 0))
    st = pl.BlockSpec((bb, GLA_HEADS, GLA_DK, GLA_DV), lambda i: (i, 0, 0, 0))
    return pl.pallas_call(
        _gla_step_kernel,
        out_shape=(jax.ShapeDtypeStruct((nb, GLA_HEADS, GLA_DV), BF16),
                   jax.ShapeDtypeStruct(state.shape, F32)),
        grid=(nb // bb,),
        in_specs=[col, col, col, val, val, st, pl.BlockSpec((1, GLA_DV), lambda i: (0, 0))],
        out_specs=(val, st),
        compiler_params=_params("parallel"),
        name="gla_step",
    )(q_t, k_t, a_t, v, r, state, gnorm)


def _log_decay_kernel(alow_ref, wup_ref, bias_ref, o_ref):
    o_ref[...] = _log_decay(alow_ref[...], wup_ref[...], bias_ref[...])


def log_decay(alow, w_up, bias):
    m = alow.shape[0]
    return pl.pallas_call(
        _log_decay_kernel,
        out_shape=jax.ShapeDtypeStruct((m, GLA_QK_WIDTH), F32),
        name="log_decay",
    )(alow, w_up, bias)


def _masked_softmax_steps(s_lists, mask_lists, carries, v_ts):
    c_exp = (MOBA_HEAD_DIM ** -0.5) * LOG2_E
    heads = range(len(s_lists))
    s_lists = [[jnp.where(mk, s, NEG_INF) for s, mk in zip(s_lists[e], mask_lists[e])] for e in heads]
    m_news = []
    for e in heads:
        m_new = carries[e][0]
        for s in s_lists[e]:
            m_new = jnp.maximum(m_new, jnp.max(s, axis=0, keepdims=True))
        m_news.append(m_new)
    p_alls, l_news, alphas = [], [], []
    for e in heads:
        m, l, _ = carries[e]
        p_list = [jnp.exp2((s - m_news[e]) * c_exp) for s in s_lists[e]]
        alpha = jnp.exp2((m - m_news[e]) * c_exp)
        l_new = alpha * l
        for p in p_list:
            l_new = l_new + jnp.sum(p, axis=0, keepdims=True)
        p_alls.append((p_list[0] if len(p_list) == 1 else jnp.concatenate(p_list, axis=0)).astype(BF16))
        l_news.append(l_new)
        alphas.append(alpha)
    pv = [_dot(v_ts[e], p_alls[e]) for e in heads]
    return tuple((m_news[e], l_news[e], alphas[e] * carries[e][2] + pv[e]) for e in heads)


MOBA_HEADS_PER_STEP = 4


def _moba_prompt_kernel(q_ref, k_ref, v_ref, o_ref, kb_ref, vt_ref, kmean_ref, sel_ref):
    c = pl.program_id(2)
    blk = MOBA_BLOCK
    hd = MOBA_HEAD_DIM
    nb = k_ref.shape[0] // blk
    heads = range(MOBA_HEADS_PER_STEP)

    @pl.when(c == 0)
    def _():
        for e in heads:
            ls = slice(e * hd, (e + 1) * hd)
            for n in range(nb):
                kn = k_ref[n * blk:(n + 1) * blk, ls]
                kmean_ref[e, n:n + 1, :] = jnp.sum(kn, axis=0, keepdims=True) * (1.0 / blk)
                kb_ref[e, n * blk:(n + 1) * blk, :] = kn.astype(BF16)
                vt_ref[e, :, n * blk:(n + 1) * blk] = v_ref[n * blk:(n + 1) * blk, ls].T.astype(BF16)

    own = pl.multiple_of(c * blk, blk)
    nq = q_ref.shape[0]
    qb_ts = []
    for e in heads:
        q_t = q_ref[:, e * hd:(e + 1) * hd].T
        qb_ts.append(q_t.astype(BF16))
        gate = _dot(kmean_ref[e], q_t, precision=lax.Precision.HIGHEST)
        blk_id = lax.broadcasted_iota(jnp.int32, gate.shape, 0)
        past = blk_id < c
        for n in range(nb):
            g_n = gate[n:n + 1, :]
            beats = ((gate > g_n) | ((gate == g_n) & (blk_id < n))) & past
            rank = jnp.sum(beats.astype(F32), axis=0, keepdims=True)
            sel_ref[e, n:n + 1, :] = jnp.where((rank < MOBA_TOPK) & (n < c), 1.0, 0.0)

    causal = lax.broadcasted_iota(jnp.int32, (blk, nq), 0) <= lax.broadcasted_iota(jnp.int32, (blk, nq), 1)
    init = (jnp.full((1, nq), NEG_INF, F32), jnp.zeros((1, nq), F32), jnp.zeros((hd, nq), F32))
    carry = _masked_softmax_steps(
        [[_dot(kb_ref[e, pl.ds(own, blk), :], qb_ts[e])] for e in heads], [[causal]] * len(heads),
        [init] * len(heads), [vt_ref[e, :, pl.ds(own, blk)] for e in heads])

    def body(j, carry):
        start = pl.multiple_of(j * (2 * blk), 2 * blk)
        s2 = [_dot(kb_ref[e, pl.ds(start, 2 * blk), :], qb_ts[e]) for e in heads]
        masks = [[sel_ref[e, pl.ds(2 * j, 1), :] > 0.0, sel_ref[e, pl.ds(2 * j + 1, 1), :] > 0.0] for e in heads]
        return _masked_softmax_steps([[s[:blk], s[blk:]] for s in s2], masks, carry,
                                     [vt_ref[e, :, pl.ds(start, 2 * blk)] for e in heads])

    carry = lax.fori_loop(0, (c + 1) // 2, body, carry)
    for e in heads:
        m, l, acc = carry[e]
        o_ref[:, e * hd:(e + 1) * hd] = (acc / l).T.astype(o_ref.dtype)


def moba_prompt(proj, batch, seq, out_rows):
    nqb = seq // MOBA_BLOCK
    hp = MOBA_HEADS_PER_STEP
    hg = MOBA_HEADS // hp
    w = hp * MOBA_HEAD_DIM
    assert nqb % 2 == 0
    return pl.pallas_call(
        _moba_prompt_kernel,
        out_shape=jax.ShapeDtypeStruct((out_rows, MOBA_WIDTH), BF16),
        grid=(batch, hg, nqb),
        in_specs=[
            pl.BlockSpec((MOBA_BLOCK, w), lambda b, h, c: (b * nqb + c, h)),
            pl.BlockSpec((seq, w), lambda b, h, c: (b, hg + h)),
            pl.BlockSpec((seq, w), lambda b, h, c: (b, 2 * hg + h)),
        ],
        out_specs=pl.BlockSpec((MOBA_BLOCK, w), lambda b, h, c: (b * nqb + c, h)),
        scratch_shapes=[pltpu.VMEM((hp, seq, MOBA_HEAD_DIM), BF16), pltpu.VMEM((hp, MOBA_HEAD_DIM, seq), BF16),
                        pltpu.VMEM((hp, nqb, MOBA_HEAD_DIM), F32), pltpu.VMEM((hp, nqb, MOBA_BLOCK), F32)],
        compiler_params=_params("parallel", "parallel", "arbitrary"),
        name="moba_prompt",
    )(proj, proj, proj)


MOBA_STEP_BLOCKS = 2


def _moba_step_kernel(pt_ref, q_ref, kn_ref, vn_ref, *refs):
    n_pages = 2 * MOBA_STEP_BLOCKS
    k_refs, v_refs = refs[:n_pages], refs[n_pages:2 * n_pages]
    o_ref, m_ref, l_ref, g_ref, acc_ref = refs[2 * n_pages:]
    step = pl.program_id(1)
    hn, hd = MOBA_HEADS, MOBA_HEAD_DIM
    page = k_refs[0].shape[1]
    scale = hd ** -0.5
    q = q_ref[0]
    qb = q.astype(BF16)
    lanes = page * hn
    same_head = (lax.broadcasted_iota(jnp.int32, (hn, lanes), 1) % hn) == lax.broadcasted_iota(jnp.int32, (hn, lanes), 0)

    blocks = range(MOBA_STEP_BLOCKS)
    s_parts = [_dot_nt(qb, k_ref[0].reshape(lanes, hd).astype(BF16)) for k_ref in k_refs]
    gsum = [jnp.sum(jnp.where(same_head, s, 0.0), axis=-1, keepdims=True) for s in s_parts]
    s_m = [jnp.where(same_head, s * scale, NEG_INF) for s in s_parts]
    s_max = [jnp.max(s, axis=-1, keepdims=True) for s in s_m]
    m = [jnp.maximum(s_max[2 * i], s_max[2 * i + 1]) for i in blocks]
    p = [jnp.exp(s - m[j // 2]) for j, s in enumerate(s_m)]
    p_sum = [jnp.sum(p_j, axis=-1, keepdims=True) for p_j in p]
    pv = [_dot(p_j.astype(BF16), v_ref[0].reshape(lanes, hd).astype(BF16)) for p_j, v_ref in zip(p, v_refs)]
    for i in blocks:
        n = step * MOBA_STEP_BLOCKS + i
        g_ref[n] = gsum[2 * i] + gsum[2 * i + 1]
        m_ref[n] = m[i]
        l_ref[n] = p_sum[2 * i] + p_sum[2 * i + 1]
        acc_ref[n] = pv[2 * i] + pv[2 * i + 1]

    @pl.when(step == pl.num_programs(1) - 1)
    def _():
        nbk = m_ref.shape[0]
        gates = [g_ref[i] for i in range(nbk)]
        s_own = jnp.sum(q * kn_ref[0], axis=-1, keepdims=True) * scale
        sels = []
        m_all = s_own
        for i in range(nbk):
            rank = jnp.zeros_like(s_own)
            for j in range(nbk):
                if j == i:
                    continue
                beats = (gates[j] > gates[i]) | ((gates[j] == gates[i]) & (j < i))
                rank += beats.astype(F32)
            sel = rank < MOBA_TOPK
            sels.append(sel)
            m_all = jnp.where(sel, jnp.maximum(m_all, m_ref[i]), m_all)
        w_own = jnp.exp(s_own - m_all)
        den = w_own
        num = w_own * vn_ref[0]
        for i in range(nbk):
            w = jnp.where(sels[i], jnp.exp(m_ref[i] - m_all), 0.0)
            den += w * l_ref[i]
            num += w * acc_ref[i]
        o_ref[0] = (num / den).astype(o_ref.dtype)


def moba_step(q, k_new, v_new, cache_k, cache_v, page_table):
    nb, n_pages = page_table.shape
    page = cache_k.shape[1]
    assert MOBA_BLOCK == 2 * page
    pages_per_step = 2 * MOBA_STEP_BLOCKS
    n_blocks = n_pages // 2
    assert n_pages % pages_per_step == 0
    hn, hd = MOBA_HEADS, MOBA_HEAD_DIM
    tok = pl.BlockSpec((1, hn, hd), lambda b, n, pt: (b, 0, 0))

    def page_spec(j):
        return pl.BlockSpec((1, page, hn, hd), lambda b, n, pt: (pt[b * n_pages + n * pages_per_step + j], 0, 0, 0))

    pages = [page_spec(j) for j in range(pages_per_step)]
    return pl.pallas_call(
        _moba_step_kernel,
        out_shape=jax.ShapeDtypeStruct((nb, hn, hd), BF16),
        grid_spec=pltpu.PrefetchScalarGridSpec(
            num_scalar_prefetch=1,
            grid=(nb, n_pages // pages_per_step),
            in_specs=[tok, tok, tok] + pages + pages,
            out_specs=tok,
            scratch_shapes=[pltpu.VMEM((n_blocks, hn, 1), F32), pltpu.VMEM((n_blocks, hn, 1), F32),
                            pltpu.VMEM((n_blocks, hn, 1), F32), pltpu.VMEM((n_blocks, hn, hd), F32)],
        ),
        compiler_params=_params("parallel", "arbitrary"),
        name="moba_step",
    )(page_table.reshape(-1), q, k_new, v_new, *([cache_k] * pages_per_step), *([cache_v] * pages_per_step))


def _row_tile(m, target):
    best = None
    for t in range(16, target + 1, 16):
        if m % t == 0:
            best = t
    assert best is not None
    return best


def kernel(x_prompt, x_sample, cache_k, cache_v, state_gla, page_table, norm_mix_pre, norm_mix_post, w_in,
           w_gla_gate_up, b_gla_gate, gla_norm, w_merge, w_branch_gla, w_branch_moba, w_out, norm_ffn_pre,
           norm_ffn_post, w_ffn_gate, w_ffn_up, w_ffn_down):
    batch, seq, d = x_prompt.shape
    nb_s = x_sample.shape[0]
    mp = batch * seq
    m_all = mp + nb_s
    gla_w = 2 * GLA_QK_WIDTH + 2 * GLA_V_WIDTH
    moba0 = gla_w + GLA_GATE_RANK
    hn, hd = MOBA_HEADS, MOBA_HEAD_DIM

    w_in_moba = w_in[:, moba0:].astype(BF16)
    w_down = w_ffn_down.astype(BF16)
    w_up_pad = jnp.pad(w_gla_gate_up, ((0, LANE - GLA_GATE_RANK), (0, 0)))
    bias = b_gla_gate.reshape(1, -1)
    gnorm = gla_norm.reshape(1, -1)

    xp = x_prompt.reshape(mp, d)
    xs = x_sample.reshape(nb_s, d)
    tm = _row_tile(m_all, 1040)
    xn = rmsnorm_groups(xp, xs, norm_mix_pre)
    p_gla = matmul(xn, w_in, tm, 512, n_cols=gla_w)
    p_low = matmul(xn, w_in, tm, LANE, n_cols=LANE, col0=gla_w // LANE)
    p_moba = matmul(xn, w_in_moba, tm, 512)

    o_g, state_p = gla_prompt(p_gla, p_low, w_up_pad, bias, gnorm, batch, seq, m_all)
    o_m = moba_prompt(p_moba, batch, seq, m_all)

    s_gla, s_moba = p_gla[mp:], p_moba[mp:]
    log_a = log_decay(p_low[mp:], w_up_pad, bias)

    def cols(t):
        return t.reshape(nb_s, GLA_HEADS, GLA_DK).transpose(0, 2, 1)

    o_gs, state_s = gla_step(
        cols(s_gla[:, :GLA_QK_WIDTH]), cols(s_gla[:, GLA_QK_WIDTH:2 * GLA_QK_WIDTH]), cols(log_a),
        s_gla[:, 2 * GLA_QK_WIDTH:2 * GLA_QK_WIDTH + GLA_V_WIDTH].reshape(nb_s, GLA_HEADS, GLA_DV),
        s_gla[:, 2 * GLA_QK_WIDTH + GLA_V_WIDTH:].reshape(nb_s, GLA_HEADS, GLA_DV),
        state_gla, gnorm, 4)
    q_s = s_moba[:, :MOBA_WIDTH].reshape(nb_s, hn, hd)
    k_s = s_moba[:, MOBA_WIDTH:2 * MOBA_WIDTH].reshape(nb_s, hn, hd)
    v_s = s_moba[:, 2 * MOBA_WIDTH:].reshape(nb_s, hn, hd)
    o_ms = moba_step(q_s, k_s, v_s, cache_k, cache_v, page_table)
    o_g = lax.dynamic_update_slice(o_g, o_gs.reshape(nb_s, GLA_V_WIDTH), (mp, 0))
    o_m = lax.dynamic_update_slice(o_m, o_ms.reshape(nb_s, MOBA_WIDTH), (mp, 0))

    mixed = gated_mix(xn, o_g, o_m, w_merge, w_branch_gla, w_branch_moba, tm, 256)
    h, h_n = norm_residual_groups(matmul(mixed, w_out, tm, 512), xp, xs, norm_mix_post, norm_ffn_pre)
    a = swiglu_up(h_n, w_ffn_gate, w_ffn_up, tm, 256)
    f = matmul(a, w_down, _row_tile(m_all, 520), 512)
    y_p = norm_residual_rows(f, h, norm_ffn_post, 0, mp)
    y_s = norm_residual_rows(f, h, norm_ffn_post, mp, nb_s)

    k_p = p_moba[:mp, MOBA_WIDTH:2 * MOBA_WIDTH].reshape(batch, seq, hn, hd)
    v_p = p_moba[:mp, 2 * MOBA_WIDTH:].reshape(batch, seq, hn, hd)
    return (y_p.reshape(batch, seq, d), y_s.reshape(nb_s, 1, d), k_p, v_p, state_p,
            k_s.reshape(nb_s, 1, hn, hd), v_s.reshape(nb_s, 1, hn, hd), state_s)
```

```python
import math

import jax
import jax.numpy as jnp
from jax import lax
from jax.experimental import pallas as pl
from jax.experimental.pallas import tpu as pltpu

F32 = jnp.float32
BF16 = jnp.bfloat16

GLA_HEADS = 8
GLA_DK = 128
GLA_DV = 256
GLA_QK_WIDTH = GLA_HEADS * GLA_DK
GLA_V_WIDTH = GLA_HEADS * GLA_DV
GLA_GATE_RANK = 16
GLA_GATE_TAU = 16.0
GLA_CHUNK = 64
GLA_SUB = 16
MOBA_HEADS = 16
MOBA_HEAD_DIM = 128
MOBA_WIDTH = MOBA_HEADS * MOBA_HEAD_DIM
MOBA_BLOCK = 256
MOBA_TOPK = 3
RMS_EPS = 1e-6
LANE = 128
VMEM_LIMIT = 56 * 1024 * 1024
NEG_INF = float("-inf")
LOG2_E = math.log2(math.e)
ROW_TILE = 128


def _params(*sem):
    return pltpu.CompilerParams(dimension_semantics=sem, vmem_limit_bytes=VMEM_LIMIT)


def _sigmoid(x):
    return 1.0 / (1.0 + jnp.exp(-x))


def _log_sigmoid(x):
    return jnp.minimum(x, 0.0) - jnp.log1p(jnp.exp(-jnp.abs(x)))


def _dot(a, b, precision=None):
    return jnp.dot(a, b, precision=precision, preferred_element_type=F32)


def _dot_nt(a, b):
    return lax.dot_general(a, b, (((1,), (1,)), ((), ())), preferred_element_type=F32)


def _dot_tn(a, b):
    return lax.dot_general(a, b, (((0,), (0,)), ((), ())), preferred_element_type=F32)


def _bf16(w):
    return w if w.dtype == BF16 else w.astype(BF16)


def _rms(x, w):
    return x * lax.rsqrt(jnp.mean(x * x, axis=-1, keepdims=True) + RMS_EPS) * w


def _group_specs(mp, ms, d):
    npb = mp // ROW_TILE
    assert mp % ROW_TILE == 0 and ms % ROW_TILE == 0
    p_spec = pl.BlockSpec((ROW_TILE, d), lambda i: (jnp.minimum(i, npb - 1), 0))
    s_spec = pl.BlockSpec((ROW_TILE, d), lambda i: (jnp.maximum(i - npb, 0), 0))
    return npb, p_spec, s_spec


def rmsnorm_groups(x_p, x_s, w):
    (mp, d), ms = x_p.shape, x_s.shape[0]
    npb, p_spec, s_spec = _group_specs(mp, ms, d)

    def kern(xp_ref, xs_ref, w_ref, o_ref):
        i = pl.program_id(0)

        @pl.when(i < npb)
        def _():
            o_ref[...] = _rms(xp_ref[...], w_ref[...]).astype(o_ref.dtype)

        @pl.when(i >= npb)
        def _():
            o_ref[...] = _rms(xs_ref[...], w_ref[...]).astype(o_ref.dtype)

    return pl.pallas_call(
        kern,
        out_shape=jax.ShapeDtypeStruct((mp + ms, d), BF16),
        grid=((mp + ms) // ROW_TILE,),
        in_specs=[p_spec, s_spec, pl.BlockSpec((1, d), lambda i: (0, 0))],
        out_specs=pl.BlockSpec((ROW_TILE, d), lambda i: (i, 0)),
        compiler_params=_params("arbitrary"),
        name="rmsnorm_groups",
    )(x_p, x_s, w.reshape(1, d))


def norm_residual_groups(f, x_p, x_s, w, w_next):
    (mp, d), ms = x_p.shape, x_s.shape[0]
    npb, p_spec, s_spec = _group_specs(mp, ms, d)
    row = pl.BlockSpec((ROW_TILE, d), lambda i: (i, 0))
    vec = pl.BlockSpec((1, d), lambda i: (0, 0))

    def kern(f_ref, xp_ref, xs_ref, w_ref, wn_ref, h_ref, hn_ref):
        i = pl.program_id(0)

        def body(x_ref):
            h = x_ref[...] + _rms(f_ref[...], w_ref[...])
            h_ref[...] = h
            hn_ref[...] = _rms(h, wn_ref[...]).astype(hn_ref.dtype)

        pl.when(i < npb)(lambda: body(xp_ref))
        pl.when(i >= npb)(lambda: body(xs_ref))

    return pl.pallas_call(
        kern,
        out_shape=(jax.ShapeDtypeStruct((mp + ms, d), F32), jax.ShapeDtypeStruct((mp + ms, d), BF16)),
        grid=((mp + ms) // ROW_TILE,),
        in_specs=[row, p_spec, s_spec, vec, vec],
        out_specs=(row, row),
        compiler_params=_params("arbitrary"),
        name="norm_residual_groups",
    )(f, x_p, x_s, w.reshape(1, d), w_next.reshape(1, d))


def norm_residual_rows(f, h, w, row0, rows):
    d = h.shape[1]
    tile = math.gcd(math.gcd(row0, rows), 2 * ROW_TILE)
    b0 = row0 // tile
    assert tile % 8 == 0

    def kern(f_ref, h_ref, w_ref, o_ref):
        o_ref[...] = h_ref[...] + _rms(f_ref[...], w_ref[...])

    src = pl.BlockSpec((tile, d), lambda i: (i + b0, 0))
    return pl.pallas_call(
        kern,
        out_shape=jax.ShapeDtypeStruct((rows, d), F32),
        grid=(rows // tile,),
        in_specs=[src, src, pl.BlockSpec((1, d), lambda i: (0, 0))],
        out_specs=pl.BlockSpec((tile, d), lambda i: (i, 0)),
        compiler_params=_params("arbitrary"),
        name="norm_residual_rows",
    )(f, h, w.reshape(1, d))


def _mm_kernel(x_ref, w_ref, o_ref):
    o_ref[...] = _dot(x_ref[...], _bf16(w_ref[...])).astype(o_ref.dtype)


def matmul(x, w, tm, tn, n_cols=None, col0=0, out_dtype=F32):
    m, k = x.shape
    n = w.shape[1] if n_cols is None else n_cols
    assert m % tm == 0
    return pl.pallas_call(
        _mm_kernel,
        out_shape=jax.ShapeDtypeStruct((m, n), out_dtype),
        grid=(m // tm, pl.cdiv(n, tn)),
        in_specs=[pl.BlockSpec((tm, k), lambda i, j: (i, 0)), pl.BlockSpec((k, tn), lambda i, j: (0, j + col0))],
        out_specs=pl.BlockSpec((tm, tn), lambda i, j: (i, j)),
        compiler_params=_params("parallel", "arbitrary"),
        name="matmul",
    )(x, w)


def _mm_nt_kernel(x_ref, w_ref, o_ref):
    o_ref[...] = _dot_nt(x_ref[...], _bf16(w_ref[...])).astype(o_ref.dtype)


def matmul_nt(x, w_t, tm, tn, row0, n_cols):
    m, k = x.shape
    assert m % tm == 0 and n_cols % tn == 0 and row0 % 8 == 0 and row0 + n_cols <= w_t.shape[0]
    return pl.pallas_call(
        _mm_nt_kernel,
        out_shape=jax.ShapeDtypeStruct((m, n_cols), F32),
        grid=(m // tm, n_cols // tn),
        in_specs=[pl.BlockSpec((tm, k), lambda i, j: (i, 0)),
                  pl.BlockSpec((pl.Element(tn), pl.Element(k)), lambda i, j: ((row0 // 8 + j * (tn // 8)) * 8, 0))],
        out_specs=pl.BlockSpec((tm, tn), lambda i, j: (i, j)),
        compiler_params=_params("parallel", "arbitrary"),
        name="matmul_nt",
    )(x, w_t)


def _mix_kernel(xn_ref, og_ref, om_ref, wga_ref, wgb_ref, wba_ref, wbb_ref, o_ref):
    xn = xn_ref[...]
    g_a = _sigmoid(_dot(xn, _bf16(wga_ref[...])))
    g_b = _sigmoid(_dot(xn, _bf16(wgb_ref[...])))
    mixed = g_a * _dot(og_ref[...], _bf16(wba_ref[...])) + g_b * _dot(om_ref[...], _bf16(wbb_ref[...]))
    o_ref[...] = mixed.astype(o_ref.dtype)


def gated_mix(xn, o_g, o_m, w_merge, w_bg, w_bm, tm, tn):
    m, d = xn.shape
    nb = d // tn
    assert m % tm == 0 and d % tn == 0

    def resident(width):
        return pl.BlockSpec((tm, width), lambda i, j: (i, 0), pipeline_mode=pl.Buffered(1))

    return pl.pallas_call(
        _mix_kernel,
        out_shape=jax.ShapeDtypeStruct((m, d), BF16),
        grid=(m // tm, nb),
        in_specs=[
            resident(d), resident(o_g.shape[1]), resident(o_m.shape[1]),
            pl.BlockSpec((d, tn), lambda i, j: (0, j)),
            pl.BlockSpec((d, tn), lambda i, j: (0, j + nb)),
            pl.BlockSpec((w_bg.shape[0], tn), lambda i, j: (0, j)),
            pl.BlockSpec((w_bm.shape[0], tn), lambda i, j: (0, j)),
        ],
        out_specs=pl.BlockSpec((tm, tn), lambda i, j: (i, j)),
        compiler_params=_params("parallel", "arbitrary"),
        name="gated_mix",
    )(xn, o_g, o_m, w_merge, w_merge, w_bg, w_bm)


def _swiglu_kernel(x_ref, wg_ref, wu_ref, o_ref):
    x = x_ref[...]
    g = _dot(x, _bf16(wg_ref[...]))
    u = _dot(x, _bf16(wu_ref[...]))
    o_ref[...] = (g * _sigmoid(g) * u).astype(o_ref.dtype)


def swiglu_up(x, w_gate, w_up, tm, tn):
    m, k = x.shape
    n = w_gate.shape[1]
    assert m % tm == 0 and n % tn == 0
    return pl.pallas_call(
        _swiglu_kernel,
        out_shape=jax.ShapeDtypeStruct((m, n), BF16),
        grid=(m // tm, n // tn),
        in_specs=[pl.BlockSpec((tm, k), lambda i, j: (i, 0)),
                  pl.BlockSpec((k, tn), lambda i, j: (0, j)),
                  pl.BlockSpec((k, tn), lambda i, j: (0, j))],
        out_specs=pl.BlockSpec((tm, tn), lambda i, j: (i, j)),
        compiler_params=_params("parallel", "arbitrary"),
        name="swiglu_up",
    )(x, w_gate, w_up)


def _gla_head(q, k, v, b, s_t):
    c = q.shape[0]
    nsub = c // GLA_SUB
    row = lax.broadcasted_iota(jnp.int32, (c, c), 0)
    col = lax.broadcasted_iota(jnp.int32, (c, c), 1)
    diff = row - col
    row_blk = row // GLA_SUB
    same_sub = (row_blk == col // GLA_SUB) & (col <= row)

    o_inter = _dot_nt((q * jnp.exp(b)).astype(BF16), s_t.astype(BF16))

    grp = 8
    assert GLA_SUB == 2 * grp and c % GLA_SUB == 0
    dk = q.shape[1]
    tiles = lambda x: x.reshape(c // grp, grp, dk)
    rot = lambda x, r: x if r == 0 else pltpu.roll(x, r, 1)
    q3, k3, b3 = tiles(q), tiles(k), tiles(b)
    a_same = jnp.zeros((c, c), F32)
    for r in range(grp):
        w = q3 * rot(k3, r) * jnp.exp(jnp.minimum(b3 - rot(b3, r), 0.0))
        a_same = jnp.where(diff == r, jnp.sum(w, axis=-1, keepdims=True).reshape(c, 1), a_same)
    halves = lambda x: x.reshape(nsub, 2, grp, dk)
    q_hi, k_lo, b_hi, b_lo = halves(q)[:, 1], halves(k)[:, 0], halves(b)[:, 1], halves(b)[:, 0]
    a_cross = jnp.zeros((c, c), F32)
    for r in range(grp):
        w = q_hi * rot(k_lo, r) * jnp.exp(b_hi - rot(b_lo, r))
        val = jnp.sum(w, axis=-1, keepdims=True)
        val = jnp.broadcast_to(val[:, None], (nsub, 2, grp, 1)).reshape(c, 1)
        a_cross = jnp.where((diff & (grp - 1)) == r, val, a_cross)
    same_grp = (row // grp == col // grp) & (col <= row)
    a_band = jnp.where(same_grp, a_same, a_cross)

    if nsub > 1:
        t_row = lax.broadcasted_iota(jnp.int32, (c, 1), 0)
        t_blk = t_row // GLA_SUB
        q_parts, k_parts = [], []
        for i in range(1, nsub):
            r_i = b[i * GLA_SUB - 1:i * GLA_SUB, :]
            q_i = jnp.where(t_blk == i, q * jnp.exp(jnp.minimum(b - r_i, 0.0)), 0.0)
            k_i = jnp.where(t_row < i * GLA_SUB, k * jnp.exp(jnp.minimum(r_i - b, 0.0)), 0.0)
            q_parts.append(q_i.astype(BF16))
            k_parts.append(k_i.astype(BF16))
        a_off = _dot_nt(jnp.concatenate(q_parts, axis=1), jnp.concatenate(k_parts, axis=1))
        a = jnp.where(same_sub, a_band, jnp.where(col < row_blk * GLA_SUB, a_off, 0.0))
    else:
        a = jnp.where(same_sub, a_band, 0.0)

    o = o_inter + _dot(a.astype(BF16), v.astype(BF16))
    b_last = b[c - 1:c, :]
    k_dec = k * jnp.exp(b_last - b)
    s_new = s_t * jnp.exp(b_last) + _dot_tn(v.astype(BF16), k_dec.astype(BF16))
    return o, s_new


def _gla_out(o, r, gnorm):
    return _rms(o, gnorm) * (r * _sigmoid(r))


def _log_decay(alow, wup, bias):
    z = _dot(alow.astype(BF16), wup.astype(BF16)) + bias
    return _log_sigmoid(z) / GLA_GATE_TAU


def _gla_prompt_kernel(q_ref, k_ref, v_ref, r_ref, alow_ref, wup_ref, bias_ref, gnorm_ref,
                       o_ref, state_ref, st_ref):
    t = pl.program_id(1)
    c = q_ref.shape[0]

    @pl.when(t == 0)
    def _():
        st_ref[...] = jnp.zeros_like(st_ref)

    log_a = _log_decay(alow_ref[...], wup_ref[...], bias_ref[...])
    tril = (lax.broadcasted_iota(jnp.int32, (c, c), 1) <= lax.broadcasted_iota(jnp.int32, (c, c), 0)).astype(F32)
    b_all = _dot(tril, log_a, precision=lax.Precision.HIGHEST)
    gnorm = gnorm_ref[...]
    scale = GLA_DK ** -0.5
    for h in range(GLA_HEADS):
        ks = slice(h * GLA_DK, (h + 1) * GLA_DK)
        vs = slice(h * GLA_DV, (h + 1) * GLA_DV)
        o, s_new = _gla_head(q_ref[:, ks] * scale, k_ref[:, ks], v_ref[:, vs], b_all[:, ks], st_ref[h])
        st_ref[h] = s_new
        o_ref[:, vs] = _gla_out(o, r_ref[:, vs], gnorm).astype(o_ref.dtype)

    @pl.when(t == pl.num_programs(1) - 1)
    def _():
        for h in range(GLA_HEADS):
            state_ref[0, h] = st_ref[h].T


def gla_prompt(proj, alow, w_up, bias, gnorm, batch, seq, out_rows):
    n_chunks = seq // GLA_CHUNK
    c = GLA_CHUNK

    def rows(b, t):
        return b * n_chunks + t

    return pl.pallas_call(
        _gla_prompt_kernel,
        out_shape=(jax.ShapeDtypeStruct((out_rows, GLA_V_WIDTH), BF16),
                   jax.ShapeDtypeStruct((batch, GLA_HEADS, GLA_DK, GLA_DV), F32)),
        grid=(batch, n_chunks),
        in_specs=[
            pl.BlockSpec((c, GLA_QK_WIDTH), lambda b, t: (rows(b, t), 0)),
            pl.BlockSpec((c, GLA_QK_WIDTH), lambda b, t: (rows(b, t), 1)),
            pl.BlockSpec((c, GLA_V_WIDTH), lambda b, t: (rows(b, t), 1)),
            pl.BlockSpec((c, GLA_V_WIDTH), lambda b, t: (rows(b, t), 2)),
            pl.BlockSpec((c, LANE), lambda b, t: (rows(b, t), 0)),
            pl.BlockSpec((LANE, GLA_QK_WIDTH), lambda b, t: (0, 0)),
            pl.BlockSpec((1, GLA_QK_WIDTH), lambda b, t: (0, 0)),
            pl.BlockSpec((1, GLA_DV), lambda b, t: (0, 0)),
        ],
        out_specs=(pl.BlockSpec((c, GLA_V_WIDTH), lambda b, t: (rows(b, t), 0)),
                   pl.BlockSpec((1, GLA_HEADS, GLA_DK, GLA_DV), lambda b, t: (b, 0, 0, 0))),
        scratch_shapes=[pltpu.VMEM((GLA_HEADS, GLA_DV, GLA_DK), F32)],
        compiler_params=_params("parallel", "arbitrary"),
        name="gla_prompt",
    )(proj, proj, proj, proj, alow, w_up, bias, gnorm)


def _gla_step_kernel(qt_ref, kt_ref, at_ref, v_ref, r_ref, s_ref, gnorm_ref, o_ref, snew_ref):
    bb = s_ref.shape[0]
    scale = GLA_DK ** -0.5
    gnorm = gnorm_ref[...]
    for i in range(bb):
        q_t = qt_ref[i] * scale
        k_t = kt_ref[i]
        a_t = jnp.exp(at_ref[i])
        v = v_ref[i]
        r = r_ref[i]
        outs = []
        for h in range(GLA_HEADS):
            s_new = s_ref[i, h] * a_t[:, h:h + 1] + k_t[:, h:h + 1] * v[h:h + 1, :]
            snew_ref[i, h] = s_new
            outs.append(jnp.sum(q_t[:, h:h + 1] * s_new, axis=0, keepdims=True))
        o = jnp.concatenate(outs, axis=0)
        o_ref[i] = _gla_out(o, r, gnorm).astype(o_ref.dtype)


def gla_step(q_t, k_t, a_t, v, r, state, gnorm, bb):
    nb = state.shape[0]
    col = pl.BlockSpec((bb, GLA_DK, GLA_HEADS), lambda i: (i, 0, 0))
    val = pl.BlockSpec((bb, GLA_HEADS, GLA_DV), lambda i: (i, 0, 0))
    st = pl.BlockSpec((bb, GLA_HEADS, GLA_DK, GLA_DV), lambda i: (i, 0, 0, 0))
    return pl.pallas_call(
        _gla_step_kernel,
        out_shape=(jax.ShapeDtypeStruct((nb, GLA_HEADS, GLA_DV), BF16),
                   jax.ShapeDtypeStruct(state.shape, F32)),
        grid=(nb // bb,),
        in_specs=[col, col, col, val, val, st, pl.BlockSpec((1, GLA_DV), lambda i: (0, 0))],
        out_specs=(val, st),
        compiler_params=_params("parallel"),
        name="gla_step",
    )(q_t, k_t, a_t, v, r, state, gnorm)


def _log_decay_kernel(alow_ref, wup_ref, bias_ref, o_ref):
    o_ref[...] = _log_decay(alow_ref[...], wup_ref[...], bias_ref[...])


def log_decay(alow, w_up, bias):
    m = alow.shape[0]
    return pl.pallas_call(
        _log_decay_kernel,
        out_shape=jax.ShapeDtypeStruct((m, GLA_QK_WIDTH), F32),
        name="log_decay",
    )(alow, w_up, bias)


def _masked_softmax_steps(s_lists, mask_lists, carries, v_ts):
    c_exp = (MOBA_HEAD_DIM ** -0.5) * LOG2_E
    heads = range(len(s_lists))
    s_lists = [[jnp.where(mk, s, NEG_INF) for s, mk in zip(s_lists[e], mask_lists[e])] for e in heads]
    m_news = []
    for e in heads:
        m_new = carries[e][0]
        for s in s_lists[e]:
            m_new = jnp.maximum(m_new, jnp.max(s, axis=0, keepdims=True))
        m_news.append(m_new)
    p_alls, l_news, alphas = [], [], []
    for e in heads:
        m, l, _ = carries[e]
        p_list = [jnp.exp2((s - m_news[e]) * c_exp) for s in s_lists[e]]
        alpha = jnp.exp2((m - m_news[e]) * c_exp)
        l_new = alpha * l
        for p in p_list:
            l_new = l_new + jnp.sum(p, axis=0, keepdims=True)
        p_alls.append((p_list[0] if len(p_list) == 1 else jnp.concatenate(p_list, axis=0)).astype(BF16))
        l_news.append(l_new)
        alphas.append(alpha)
    pv = [_dot(v_ts[e], p_alls[e]) for e in heads]
    return tuple((m_news[e], l_news[e], alphas[e] * carries[e][2] + pv[e]) for e in heads)


MOBA_HEADS_PER_STEP = 4
MOBA_KV_BLOCKS_PER_ITER = 4


def _moba_prompt_kernel(q_ref, k_ref, v_ref, o_ref, kb_ref, vt_ref, kmean_ref, sel_ref):
    c = pl.program_id(2)
    blk = MOBA_BLOCK
    hd = MOBA_HEAD_DIM
    nb = k_ref.shape[0] // blk
    heads = range(MOBA_HEADS_PER_STEP)

    @pl.when(c == 0)
    def _():
        for e in heads:
            ls = slice(e * hd, (e + 1) * hd)
            for n in range(nb):
                kn = k_ref[n * blk:(n + 1) * blk, ls]
                kmean_ref[e, n:n + 1, :] = jnp.sum(kn, axis=0, keepdims=True) * (1.0 / blk)
                kb_ref[e, n * blk:(n + 1) * blk, :] = kn.astype(BF16)
                vt_ref[e, :, n * blk:(n + 1) * blk] = v_ref[n * blk:(n + 1) * blk, ls].T.astype(BF16)

    own = pl.multiple_of(c * blk, blk)
    nq = q_ref.shape[0]
    qb_ts = []
    for e in heads:
        q_t = q_ref[:, e * hd:(e + 1) * hd].T
        qb_ts.append(q_t.astype(BF16))
        gate = _dot(kmean_ref[e], q_t, precision=lax.Precision.HIGHEST)
        blk_id = lax.broadcasted_iota(jnp.int32, gate.shape, 0)
        past = blk_id < c
        for n in range(nb):
            g_n = gate[n:n + 1, :]
            beats = ((gate > g_n) | ((gate == g_n) & (blk_id < n))) & past
            rank = jnp.sum(beats.astype(F32), axis=0, keepdims=True)
            sel_ref[e, n:n + 1, :] = jnp.where((rank < MOBA_TOPK) & (n < c), 1.0, 0.0)

    causal = lax.broadcasted_iota(jnp.int32, (blk, nq), 0) <= lax.broadcasted_iota(jnp.int32, (blk, nq), 1)
    init = (jnp.full((1, nq), NEG_INF, F32), jnp.zeros((1, nq), F32), jnp.zeros((hd, nq), F32))
    carry = _masked_softmax_steps(
        [[_dot(kb_ref[e, pl.ds(own, blk), :], qb_ts[e])] for e in heads], [[causal]] * len(heads),
        [init] * len(heads), [vt_ref[e, :, pl.ds(own, blk)] for e in heads])

    g = MOBA_KV_BLOCKS_PER_ITER

    def body(j, carry):
        start = pl.multiple_of(j * (g * blk), g * blk)
        s_g = [_dot(kb_ref[e, pl.ds(start, g * blk), :], qb_ts[e]) for e in heads]
        masks = [[sel_ref[e, pl.ds(g * j + i, 1), :] > 0.0 for i in range(g)] for e in heads]
        return _masked_softmax_steps([[s[i * blk:(i + 1) * blk] for i in range(g)] for s in s_g], masks, carry,
                                     [vt_ref[e, :, pl.ds(start, g * blk)] for e in heads])

    carry = lax.fori_loop(0, (c + g - 1) // g, body, carry)
    for e in heads:
        m, l, acc = carry[e]
        o_ref[:, e * hd:(e + 1) * hd] = (acc / l).T.astype(o_ref.dtype)


def moba_prompt(proj, batch, seq, out_rows):
    nqb = seq // MOBA_BLOCK
    hp = MOBA_HEADS_PER_STEP
    hg = MOBA_HEADS // hp
    w = hp * MOBA_HEAD_DIM
    assert nqb % MOBA_KV_BLOCKS_PER_ITER == 0
    return pl.pallas_call(
        _moba_prompt_kernel,
        out_shape=jax.ShapeDtypeStruct((out_rows, MOBA_WIDTH), BF16),
        grid=(batch, hg, nqb),
        in_specs=[
            pl.BlockSpec((MOBA_BLOCK, w), lambda b, h, c: (b * nqb + c, h)),
            pl.BlockSpec((seq, w), lambda b, h, c: (b, hg + h)),
            pl.BlockSpec((seq, w), lambda b, h, c: (b, 2 * hg + h)),
        ],
        out_specs=pl.BlockSpec((MOBA_BLOCK, w), lambda b, h, c: (b * nqb + c, h)),
        scratch_shapes=[pltpu.VMEM((hp, seq, MOBA_HEAD_DIM), BF16), pltpu.VMEM((hp, MOBA_HEAD_DIM, seq), BF16),
                        pltpu.VMEM((hp, nqb, MOBA_HEAD_DIM), F32), pltpu.VMEM((hp, nqb, MOBA_BLOCK), F32)],
        compiler_params=_params("parallel", "parallel", "arbitrary"),
        name="moba_prompt",
    )(proj, proj, proj)


MOBA_STEP_BLOCKS = 2


def _moba_step_kernel(pt_ref, q_ref, kn_ref, vn_ref, *refs):
    n_pages = 2 * MOBA_STEP_BLOCKS
    k_refs, v_refs = refs[:n_pages], refs[n_pages:2 * n_pages]
    o_ref, m_ref, l_ref, g_ref, acc_ref = refs[2 * n_pages:]
    step = pl.program_id(1)
    hn, hd = MOBA_HEADS, MOBA_HEAD_DIM
    page = k_refs[0].shape[1]
    scale = hd ** -0.5
    q = q_ref[0]
    qb = q.astype(BF16)
    lanes = page * hn
    same_head = (lax.broadcasted_iota(jnp.int32, (hn, lanes), 1) % hn) == lax.broadcasted_iota(jnp.int32, (hn, lanes), 0)

    blocks = range(MOBA_STEP_BLOCKS)
    s_parts = [_dot_nt(qb, k_ref[0].reshape(lanes, hd).astype(BF16)) for k_ref in k_refs]
    gsum = [jnp.sum(jnp.where(same_head, s, 0.0), axis=-1, keepdims=True) for s in s_parts]
    s_m = [jnp.where(same_head, s * scale, NEG_INF) for s in s_parts]
    s_max = [jnp.max(s, axis=-1, keepdims=True) for s in s_m]
    m = [jnp.maximum(s_max[2 * i], s_max[2 * i + 1]) for i in blocks]
    p = [jnp.exp(s - m[j // 2]) for j, s in enumerate(s_m)]
    p_sum = [jnp.sum(p_j, axis=-1, keepdims=True) for p_j in p]
    pv = [_dot(p_j.astype(BF16), v_ref[0].reshape(lanes, hd).astype(BF16)) for p_j, v_ref in zip(p, v_refs)]
    for i in blocks:
        n = step * MOBA_STEP_BLOCKS + i
        g_ref[n] = gsum[2 * i] + gsum[2 * i + 1]
        m_ref[n] = m[i]
        l_ref[n] = p_sum[2 * i] + p_sum[2 * i + 1]
        acc_ref[n] = pv[2 * i] + pv[2 * i + 1]

    @pl.when(step == pl.num_programs(1) - 1)
    def _():
        nbk = m_ref.shape[0]
        gates = [g_ref[i] for i in range(nbk)]
        s_own = jnp.sum(q * kn_ref[0], axis=-1, keepdims=True) * scale
        sels = []
        m_all = s_own
        for i in range(nbk):
            rank = jnp.zeros_like(s_own)
            for j in range(nbk):
                if j == i:
                    continue
                beats = (gates[j] > gates[i]) | ((gates[j] == gates[i]) & (j < i))
                rank += beats.astype(F32)
            sel = rank < MOBA_TOPK
            sels.append(sel)
            m_all = jnp.where(sel, jnp.maximum(m_all, m_ref[i]), m_all)
        w_own = jnp.exp(s_own - m_all)
        den = w_own
        num = w_own * vn_ref[0]
        for i in range(nbk):
            w = jnp.where(sels[i], jnp.exp(m_ref[i] - m_all), 0.0)
            den += w * l_ref[i]
            num += w * acc_ref[i]
        o_ref[0] = (num / den).astype(o_ref.dtype)


def moba_step(q, k_new, v_new, cache_k, cache_v, page_table):
    nb, n_pages = page_table.shape
    page = cache_k.shape[1]
    assert MOBA_BLOCK == 2 * page
    pages_per_step = 2 * MOBA_STEP_BLOCKS
    n_blocks = n_pages // 2
    assert n_pages % pages_per_step == 0
    hn, hd = MOBA_HEADS, MOBA_HEAD_DIM
    tok = pl.BlockSpec((1, hn, hd), lambda b, n, pt: (b, 0, 0))

    def page_spec(j):
        return pl.BlockSpec((1, page, hn, hd), lambda b, n, pt: (pt[b * n_pages + n * pages_per_step + j], 0, 0, 0))

    pages = [page_spec(j) for j in range(pages_per_step)]
    return pl.pallas_call(
        _moba_step_kernel,
        out_shape=jax.ShapeDtypeStruct((nb, hn, hd), BF16),
        grid_spec=pltpu.PrefetchScalarGridSpec(
            num_scalar_prefetch=1,
            grid=(nb, n_pages // pages_per_step),
            in_specs=[tok, tok, tok] + pages + pages,
            out_specs=tok,
            scratch_shapes=[pltpu.VMEM((n_blocks, hn, 1), F32), pltpu.VMEM((n_blocks, hn, 1), F32),
                            pltpu.VMEM((n_blocks, hn, 1), F32), pltpu.VMEM((n_blocks, hn, hd), F32)],
        ),
        compiler_params=_params("parallel", "arbitrary"),
        name="moba_step",
    )(page_table.reshape(-1), q, k_new, v_new, *([cache_k] * pages_per_step), *([cache_v] * pages_per_step))


def _row_tile(m, target):
    best = None
    for t in range(16, target + 1, 16):
        if m % t == 0:
            best = t
    assert best is not None
    return best


def kernel(x_prompt, x_sample, cache_k, cache_v, state_gla, page_table, norm_mix_pre, norm_mix_post, w_in,
           w_gla_gate_up, b_gla_gate, gla_norm, w_merge, w_branch_gla, w_branch_moba, w_out, norm_ffn_pre,
           norm_ffn_post, w_ffn_gate, w_ffn_up, w_ffn_down):
    batch, seq, d = x_prompt.shape
    nb_s = x_sample.shape[0]
    mp = batch * seq
    m_all = mp + nb_s
    gla_w = 2 * GLA_QK_WIDTH + 2 * GLA_V_WIDTH
    moba0 = gla_w + GLA_GATE_RANK
    hn, hd = MOBA_HEADS, MOBA_HEAD_DIM

    w_in_t = w_in.T
    w_down = w_ffn_down.astype(BF16)
    w_up_pad = jnp.pad(w_gla_gate_up, ((0, LANE - GLA_GATE_RANK), (0, 0)))
    bias = b_gla_gate.reshape(1, -1)
    gnorm = gla_norm.reshape(1, -1)

    xp = x_prompt.reshape(mp, d)
    xs = x_sample.reshape(nb_s, d)
    tm = _row_tile(m_all, 1040)
    xn = rmsnorm_groups(xp, xs, norm_mix_pre)
    p_gla = matmul_nt(xn, w_in_t, tm, 512, 0, gla_w)
    p_low = matmul_nt(xn, w_in_t, tm, LANE, gla_w, LANE)
    p_moba = matmul_nt(xn, w_in_t, tm, 512, moba0, 3 * MOBA_WIDTH)

    o_g, state_p = gla_prompt(p_gla, p_low, w_up_pad, bias, gnorm, batch, seq, m_all)
    o_m = moba_prompt(p_moba, batch, seq, m_all)

    s_gla, s_moba = p_gla[mp:], p_moba[mp:]
    log_a = log_decay(p_low[mp:], w_up_pad, bias)

    def cols(t):
        return t.reshape(nb_s, GLA_HEADS, GLA_DK).transpose(0, 2, 1)

    o_gs, state_s = gla_step(
        cols(s_gla[:, :GLA_QK_WIDTH]), cols(s_gla[:, GLA_QK_WIDTH:2 * GLA_QK_WIDTH]), cols(log_a),
        s_gla[:, 2 * GLA_QK_WIDTH:2 * GLA_QK_WIDTH + GLA_V_WIDTH].reshape(nb_s, GLA_HEADS, GLA_DV),
        s_gla[:, 2 * GLA_QK_WIDTH + GLA_V_WIDTH:].reshape(nb_s, GLA_HEADS, GLA_DV),
        state_gla, gnorm, 4)
    q_s = s_moba[:, :MOBA_WIDTH].reshape(nb_s, hn, hd)
    k_s = s_moba[:, MOBA_WIDTH:2 * MOBA_WIDTH].reshape(nb_s, hn, hd)
    v_s = s_moba[:, 2 * MOBA_WIDTH:].reshape(nb_s, hn, hd)
    o_ms = moba_step(q_s, k_s, v_s, cache_k, cache_v, page_table)
    o_g = lax.dynamic_update_slice(o_g, o_gs.reshape(nb_s, GLA_V_WIDTH), (mp, 0))
    o_m = lax.dynamic_update_slice(o_m, o_ms.reshape(nb_s, MOBA_WIDTH), (mp, 0))

    mixed = gated_mix(xn, o_g, o_m, w_merge, w_branch_gla, w_branch_moba, tm, 256)
    h, h_n = norm_residual_groups(matmul(mixed, w_out, tm, 512), xp, xs, norm_mix_post, norm_ffn_pre)
    a = swiglu_up(h_n, w_ffn_gate, w_ffn_up, tm, 256)
    f = matmul(a, w_down, _row_tile(m_all, 520), 512)
    y_p = norm_residual_rows(f, h, norm_ffn_post, 0, mp)
    y_s = norm_residual_rows(f, h, norm_ffn_post, mp, nb_s)

    k_p = p_moba[:mp, MOBA_WIDTH:2 * MOBA_WIDTH].reshape(batch, seq, hn, hd)
    v_p = p_moba[:mp, 2 * MOBA_WIDTH:].reshape(batch, seq, hn, hd)
    return (y_p.reshape(batch, seq, d), y_s.reshape(nb_s, 1, d), k_p, v_p, state_p,
            k_s.reshape(nb_s, 1, hn, hd), v_s.reshape(nb_s, 1, hn, hd), state_s)
```

```python
import math

import jax
import jax.numpy as jnp
from jax import lax
from jax.experimental import pallas as pl
from jax.experimental.pallas import tpu as pltpu

F32 = jnp.float32
BF16 = jnp.bfloat16

GLA_HEADS = 8
GLA_DK = 128
GLA_DV = 256
GLA_QK_WIDTH = GLA_HEADS * GLA_DK
GLA_V_WIDTH = GLA_HEADS * GLA_DV
GLA_GATE_RANK = 16
GLA_GATE_TAU = 16.0
GLA_CHUNK = 64
GLA_SUB = 16
MOBA_HEADS = 16
MOBA_HEAD_DIM = 128
MOBA_WIDTH = MOBA_HEADS * MOBA_HEAD_DIM
MOBA_BLOCK = 256
MOBA_TOPK = 3
RMS_EPS = 1e-6
LANE = 128
VMEM_LIMIT = 56 * 1024 * 1024
NEG_INF = float("-inf")
LOG2_E = math.log2(math.e)
ROW_TILE = 128


def _params(*sem):
    return pltpu.CompilerParams(dimension_semantics=sem, vmem_limit_bytes=VMEM_LIMIT)


def _sigmoid(x):
    return 1.0 / (1.0 + jnp.exp(-x))


def _log_sigmoid(x):
    return jnp.minimum(x, 0.0) - jnp.log1p(jnp.exp(-jnp.abs(x)))


def _dot(a, b, precision=None):
    return jnp.dot(a, b, precision=precision, preferred_element_type=F32)


def _dot_nt(a, b):
    return lax.dot_general(a, b, (((1,), (1,)), ((), ())), preferred_element_type=F32)


def _dot_tn(a, b):
    return lax.dot_general(a, b, (((0,), (0,)), ((), ())), preferred_element_type=F32)


def _bf16(w):
    return w if w.dtype == BF16 else w.astype(BF16)


def _rms(x, w):
    return x * lax.rsqrt(jnp.mean(x * x, axis=-1, keepdims=True) + RMS_EPS) * w


def _group_specs(mp, ms, d):
    npb = mp // ROW_TILE
    assert mp % ROW_TILE == 0 and ms % ROW_TILE == 0
    p_spec = pl.BlockSpec((ROW_TILE, d), lambda i: (jnp.minimum(i, npb - 1), 0))
    s_spec = pl.BlockSpec((ROW_TILE, d), lambda i: (jnp.maximum(i - npb, 0), 0))
    return npb, p_spec, s_spec


def rmsnorm_groups(x_p, x_s, w):
    (mp, d), ms = x_p.shape, x_s.shape[0]
    npb, p_spec, s_spec = _group_specs(mp, ms, d)

    def kern(xp_ref, xs_ref, w_ref, o_ref):
        i = pl.program_id(0)

        @pl.when(i < npb)
        def _():
            o_ref[...] = _rms(xp_ref[...], w_ref[...]).astype(o_ref.dtype)

        @pl.when(i >= npb)
        def _():
            o_ref[...] = _rms(xs_ref[...], w_ref[...]).astype(o_ref.dtype)

    return pl.pallas_call(
        kern,
        out_shape=jax.ShapeDtypeStruct((mp + ms, d), BF16),
        grid=((mp + ms) // ROW_TILE,),
        in_specs=[p_spec, s_spec, pl.BlockSpec((1, d), lambda i: (0, 0))],
        out_specs=pl.BlockSpec((ROW_TILE, d), lambda i: (i, 0)),
        compiler_params=_params("arbitrary"),
        name="rmsnorm_groups",
    )(x_p, x_s, w.reshape(1, d))


def norm_residual_groups(f, x_p, x_s, w, w_next):
    (mp, d), ms = x_p.shape, x_s.shape[0]
    npb, p_spec, s_spec = _group_specs(mp, ms, d)
    row = pl.BlockSpec((ROW_TILE, d), lambda i: (i, 0))
    vec = pl.BlockSpec((1, d), lambda i: (0, 0))

    def kern(f_ref, xp_ref, xs_ref, w_ref, wn_ref, h_ref, hn_ref):
        i = pl.program_id(0)

        def body(x_ref):
            h = x_ref[...] + _rms(f_ref[...], w_ref[...])
            h_ref[...] = h
            hn_ref[...] = _rms(h, wn_ref[...]).astype(hn_ref.dtype)

        pl.when(i < npb)(lambda: body(xp_ref))
        pl.when(i >= npb)(lambda: body(xs_ref))

    return pl.pallas_call(
        kern,
        out_shape=(jax.ShapeDtypeStruct((mp + ms, d), F32), jax.ShapeDtypeStruct((mp + ms, d), BF16)),
        grid=((mp + ms) // ROW_TILE,),
        in_specs=[row, p_spec, s_spec, vec, vec],
        out_specs=(row, row),
        compiler_params=_params("arbitrary"),
        name="norm_residual_groups",
    )(f, x_p, x_s, w.reshape(1, d), w_next.reshape(1, d))


def norm_residual_rows(f, h, w, row0, rows):
    d = h.shape[1]
    tile = math.gcd(math.gcd(row0, rows), 2 * ROW_TILE)
    b0 = row0 // tile
    assert tile % 8 == 0

    def kern(f_ref, h_ref, w_ref, o_ref):
        o_ref[...] = h_ref[...] + _rms(f_ref[...], w_ref[...])

    src = pl.BlockSpec((tile, d), lambda i: (i + b0, 0))
    return pl.pallas_call(
        kern,
        out_shape=jax.ShapeDtypeStruct((rows, d), F32),
        grid=(rows // tile,),
        in_specs=[src, src, pl.BlockSpec((1, d), lambda i: (0, 0))],
        out_specs=pl.BlockSpec((tile, d), lambda i: (i, 0)),
        compiler_params=_params("arbitrary"),
        name="norm_residual_rows",
    )(f, h, w.reshape(1, d))


def _mm_kernel(x_ref, w_ref, o_ref):
    o_ref[...] = _dot(x_ref[...], _bf16(w_ref[...])).astype(o_ref.dtype)


def matmul(x, w, tm, tn, n_cols=None, col0=0, out_dtype=F32):
    m, k = x.shape
    n = w.shape[1] if n_cols is None else n_cols
    assert m % tm == 0
    return pl.pallas_call(
        _mm_kernel,
        out_shape=jax.ShapeDtypeStruct((m, n), out_dtype),
        grid=(m // tm, pl.cdiv(n, tn)),
        in_specs=[pl.BlockSpec((tm, k), lambda i, j: (i, 0)), pl.BlockSpec((k, tn), lambda i, j: (0, j + col0))],
        out_specs=pl.BlockSpec((tm, tn), lambda i, j: (i, j)),
        compiler_params=_params("parallel", "arbitrary"),
        name="matmul",
    )(x, w)


def _mm_nt_kernel(x_ref, w_ref, o_ref):
    o_ref[...] = _dot_nt(x_ref[...], _bf16(w_ref[...])).astype(o_ref.dtype)


def _mm_nt_side_kernel(x_ref, w_ref, ws_ref, o_ref, os_ref):
    o_ref[...] = _dot_nt(x_ref[...], _bf16(w_ref[...])).astype(o_ref.dtype)

    @pl.when(pl.program_id(1) == 0)
    def _():
        os_ref[...] = _dot_nt(x_ref[...], _bf16(ws_ref[...])).astype(os_ref.dtype)


def matmul_nt(x, w_t, tm, tn, row0, n_cols, side_row0=None):
    m, k = x.shape
    assert m % tm == 0 and n_cols % tn == 0 and row0 % 8 == 0 and row0 + n_cols <= w_t.shape[0]

    def w_rows(size, start, per_step):
        return pl.BlockSpec((pl.Element(size), pl.Element(k)),
                            lambda i, j: ((start // 8 + j * (per_step // 8)) * 8, 0))

    x_spec = pl.BlockSpec((tm, k), lambda i, j: (i, 0))
    o_spec = pl.BlockSpec((tm, tn), lambda i, j: (i, j))
    if side_row0 is None:
        return pl.pallas_call(
            _mm_nt_kernel,
            out_shape=jax.ShapeDtypeStruct((m, n_cols), F32),
            grid=(m // tm, n_cols // tn),
            in_specs=[x_spec, w_rows(tn, row0, tn)],
            out_specs=o_spec,
            compiler_params=_params("parallel", "arbitrary"),
            name="matmul_nt",
        )(x, w_t)
    assert side_row0 % 8 == 0 and side_row0 + LANE <= w_t.shape[0]
    return pl.pallas_call(
        _mm_nt_side_kernel,
        out_shape=(jax.ShapeDtypeStruct((m, n_cols), F32), jax.ShapeDtypeStruct((m, LANE), F32)),
        grid=(m // tm, n_cols // tn),
        in_specs=[x_spec, w_rows(tn, row0, tn), w_rows(LANE, side_row0, 0)],
        out_specs=(o_spec, pl.BlockSpec((tm, LANE), lambda i, j: (i, 0))),
        compiler_params=_params("parallel", "arbitrary"),
        name="matmul_nt_side",
    )(x, w_t, w_t)


def _mix_kernel(xn_ref, og_ref, om_ref, wga_ref, wgb_ref, wba_ref, wbb_ref, o_ref):
    xn = xn_ref[...]
    g_a = _sigmoid(_dot(xn, _bf16(wga_ref[...])))
    g_b = _sigmoid(_dot(xn, _bf16(wgb_ref[...])))
    mixed = g_a * _dot(og_ref[...], _bf16(wba_ref[...])) + g_b * _dot(om_ref[...], _bf16(wbb_ref[...]))
    o_ref[...] = mixed.astype(o_ref.dtype)


def gated_mix(xn, o_g, o_m, w_merge, w_bg, w_bm, tm, tn):
    m, d = xn.shape
    nb = d // tn
    assert m % tm == 0 and d % tn == 0

    def resident(width):
        return pl.BlockSpec((tm, width), lambda i, j: (i, 0), pipeline_mode=pl.Buffered(1))

    return pl.pallas_call(
        _mix_kernel,
        out_shape=jax.ShapeDtypeStruct((m, d), BF16),
        grid=(m // tm, nb),
        in_specs=[
            resident(d), resident(o_g.shape[1]), resident(o_m.shape[1]),
            pl.BlockSpec((d, tn), lambda i, j: (0, j)),
            pl.BlockSpec((d, tn), lambda i, j: (0, j + nb)),
            pl.BlockSpec((w_bg.shape[0], tn), lambda i, j: (0, j)),
            pl.BlockSpec((w_bm.shape[0], tn), lambda i, j: (0, j)),
        ],
        out_specs=pl.BlockSpec((tm, tn), lambda i, j: (i, j)),
        compiler_params=_params("parallel", "arbitrary"),
        name="gated_mix",
    )(xn, o_g, o_m, w_merge, w_merge, w_bg, w_bm)


def _swiglu_kernel(x_ref, wg_ref, wu_ref, o_ref):
    x = x_ref[...]
    g = _dot(x, _bf16(wg_ref[...]))
    u = _dot(x, _bf16(wu_ref[...]))
    o_ref[...] = (g * _sigmoid(g) * u).astype(o_ref.dtype)


def swiglu_up(x, w_gate, w_up, tm, tn):
    m, k = x.shape
    n = w_gate.shape[1]
    assert m % tm == 0 and n % tn == 0
    return pl.pallas_call(
        _swiglu_kernel,
        out_shape=jax.ShapeDtypeStruct((m, n), BF16),
        grid=(m // tm, n // tn),
        in_specs=[pl.BlockSpec((tm, k), lambda i, j: (i, 0)),
                  pl.BlockSpec((k, tn), lambda i, j: (0, j)),
                  pl.BlockSpec((k, tn), lambda i, j: (0, j))],
        out_specs=pl.BlockSpec((tm, tn), lambda i, j: (i, j)),
        compiler_params=_params("parallel", "arbitrary"),
        name="swiglu_up",
    )(x, w_gate, w_up)


def _gla_head(q, k, v, b, s_t):
    c = q.shape[0]
    nsub = c // GLA_SUB
    row = lax.broadcasted_iota(jnp.int32, (c, c), 0)
    col = lax.broadcasted_iota(jnp.int32, (c, c), 1)
    diff = row - col
    row_blk = row // GLA_SUB
    same_sub = (row_blk == col // GLA_SUB) & (col <= row)

    o_inter = _dot_nt((q * jnp.exp(b)).astype(BF16), s_t.astype(BF16))

    grp = 8
    assert GLA_SUB == 2 * grp and c % GLA_SUB == 0
    dk = q.shape[1]
    tiles = lambda x: x.reshape(c // grp, grp, dk)
    rot = lambda x, r: x if r == 0 else pltpu.roll(x, r, 1)
    q3, k3, b3 = tiles(q), tiles(k), tiles(b)
    a_same = jnp.zeros((c, c), F32)
    for r in range(grp):
        w = q3 * rot(k3, r) * jnp.exp(jnp.minimum(b3 - rot(b3, r), 0.0))
        a_same = jnp.where(diff == r, jnp.sum(w, axis=-1, keepdims=True).reshape(c, 1), a_same)
    halves = lambda x: x.reshape(nsub, 2, grp, dk)
    q_hi, k_lo, b_hi, b_lo = halves(q)[:, 1], halves(k)[:, 0], halves(b)[:, 1], halves(b)[:, 0]
    a_cross = jnp.zeros((c, c), F32)
    for r in range(grp):
        w = q_hi * rot(k_lo, r) * jnp.exp(b_hi - rot(b_lo, r))
        val = jnp.sum(w, axis=-1, keepdims=True)
        val = jnp.broadcast_to(val[:, None], (nsub, 2, grp, 1)).reshape(c, 1)
        a_cross = jnp.where((diff & (grp - 1)) == r, val, a_cross)
    same_grp = (row // grp == col // grp) & (col <= row)
    a_band = jnp.where(same_grp, a_same, a_cross)

    if nsub > 1:
        t_row = lax.broadcasted_iota(jnp.int32, (c, 1), 0)
        t_blk = t_row // GLA_SUB
        q_parts, k_parts = [], []
        for i in range(1, nsub):
            r_i = b[i * GLA_SUB - 1:i * GLA_SUB, :]
            q_i = jnp.where(t_blk == i, q * jnp.exp(jnp.minimum(b - r_i, 0.0)), 0.0)
            k_i = jnp.where(t_row < i * GLA_SUB, k * jnp.exp(jnp.minimum(r_i - b, 0.0)), 0.0)
            q_parts.append(q_i.astype(BF16))
            k_parts.append(k_i.astype(BF16))
        a_off = _dot_nt(jnp.concatenate(q_parts, axis=1), jnp.concatenate(k_parts, axis=1))
        a = jnp.where(same_sub, a_band, jnp.where(col < row_blk * GLA_SUB, a_off, 0.0))
    else:
        a = jnp.where(same_sub, a_band, 0.0)

    o = o_inter + _dot(a.astype(BF16), v.astype(BF16))
    b_last = b[c - 1:c, :]
    k_dec = k * jnp.exp(b_last - b)
    s_new = s_t * jnp.exp(b_last) + _dot_tn(v.astype(BF16), k_dec.astype(BF16))
    return o, s_new


def _gla_out(o, r, gnorm):
    return _rms(o, gnorm) * (r * _sigmoid(r))


def _log_decay(alow, wup, bias):
    z = _dot(alow.astype(BF16), wup.astype(BF16)) + bias
    return _log_sigmoid(z) / GLA_GATE_TAU


def _gla_prompt_kernel(q_ref, k_ref, v_ref, r_ref, alow_ref, wup_ref, bias_ref, gnorm_ref, o_init_ref,
                       o_ref, state_ref, st_ref):
    del o_init_ref
    t = pl.program_id(1)
    c = q_ref.shape[0]

    @pl.when(t == 0)
    def _():
        st_ref[...] = jnp.zeros_like(st_ref)

    log_a = _log_decay(alow_ref[...], wup_ref[...], bias_ref[...])
    tril = (lax.broadcasted_iota(jnp.int32, (c, c), 1) <= lax.broadcasted_iota(jnp.int32, (c, c), 0)).astype(F32)
    b_all = _dot(tril, log_a, precision=lax.Precision.HIGHEST)
    gnorm = gnorm_ref[...]
    scale = GLA_DK ** -0.5
    for h in range(GLA_HEADS):
        ks = slice(h * GLA_DK, (h + 1) * GLA_DK)
        vs = slice(h * GLA_DV, (h + 1) * GLA_DV)
        o, s_new = _gla_head(q_ref[:, ks] * scale, k_ref[:, ks], v_ref[:, vs], b_all[:, ks], st_ref[h])
        st_ref[h] = s_new
        o_ref[:, vs] = _gla_out(o, r_ref[:, vs], gnorm).astype(o_ref.dtype)

    @pl.when(t == pl.num_programs(1) - 1)
    def _():
        for h in range(GLA_HEADS):
            state_ref[0, h] = st_ref[h].T


def gla_prompt(proj, alow, w_up, bias, gnorm, batch, seq, out_rows):
    n_chunks = seq // GLA_CHUNK
    c = GLA_CHUNK

    def rows(b, t):
        return b * n_chunks + t

    return pl.pallas_call(
        _gla_prompt_kernel,
        out_shape=(jax.ShapeDtypeStruct((out_rows, GLA_V_WIDTH), BF16),
                   jax.ShapeDtypeStruct((batch, GLA_HEADS, GLA_DK, GLA_DV), F32)),
        grid=(batch, n_chunks),
        in_specs=[
            pl.BlockSpec((c, GLA_QK_WIDTH), lambda b, t: (rows(b, t), 0)),
            pl.BlockSpec((c, GLA_QK_WIDTH), lambda b, t: (rows(b, t), 1)),
            pl.BlockSpec((c, GLA_V_WIDTH), lambda b, t: (rows(b, t), 1)),
            pl.BlockSpec((c, GLA_V_WIDTH), lambda b, t: (rows(b, t), 2)),
            pl.BlockSpec((c, LANE), lambda b, t: (rows(b, t), 0)),
            pl.BlockSpec((LANE, GLA_QK_WIDTH), lambda b, t: (0, 0)),
            pl.BlockSpec((1, GLA_QK_WIDTH), lambda b, t: (0, 0)),
            pl.BlockSpec((1, GLA_DV), lambda b, t: (0, 0)),
            pl.BlockSpec(memory_space=pl.ANY),
        ],
        out_specs=(pl.BlockSpec((c, GLA_V_WIDTH), lambda b, t: (rows(b, t), 0)),
                   pl.BlockSpec((1, GLA_HEADS, GLA_DK, GLA_DV), lambda b, t: (b, 0, 0, 0))),
        scratch_shapes=[pltpu.VMEM((GLA_HEADS, GLA_DV, GLA_DK), F32)],
        input_output_aliases={8: 0},
        compiler_params=_params("parallel", "arbitrary"),
        name="gla_prompt",
    )(proj, proj, proj, proj, alow, w_up, bias, gnorm, jnp.zeros((out_rows, GLA_V_WIDTH), BF16))


def _gla_step_kernel(qt_ref, kt_ref, at_ref, v_ref, r_ref, s_ref, gnorm_ref, o_ref, snew_ref):
    bb = s_ref.shape[0]
    scale = GLA_DK ** -0.5
    gnorm = gnorm_ref[...]
    for i in range(bb):
        q_t = qt_ref[i] * scale
        k_t = kt_ref[i]
        a_t = jnp.exp(at_ref[i])
        v = v_ref[i]
        r = r_ref[i]
        outs = []
        for h in range(GLA_HEADS):
            s_new = s_ref[i, h] * a_t[:, h:h + 1] + k_t[:, h:h + 1] * v[h:h + 1, :]
            snew_ref[i, h] = s_new
            outs.append(jnp.sum(q_t[:, h:h + 1] * s_new, axis=0, keepdims=True))
        o = jnp.concatenate(outs, axis=0)
        o_ref[i] = _gla_out(o, r, gnorm).astype(o_ref.dtype)


def gla_step(q_t, k_t, a_t, v, r, state, gnorm, bb):
    nb = state.shape[0]
    col = pl.BlockSpec((bb, GLA_DK, GLA_HEADS), lambda i: (i, 0, 0))
    val = pl.BlockSpec((bb, GLA_HEADS, GLA_DV), lambda i: (i, 0, 0))
    st = pl.BlockSpec((bb, GLA_HEADS, GLA_DK, GLA_DV), lambda i: (i, 0, 0, 0))
    return pl.pallas_call(
        _gla_step_kernel,
        out_shape=(jax.ShapeDtypeStruct((nb, GLA_HEADS, GLA_DV), BF16),
                   jax.ShapeDtypeStruct(state.shape, F32)),
        grid=(nb // bb,),
        in_specs=[col, col, col, val, val, st, pl.BlockSpec((1, GLA_DV), lambda i: (0, 0))],
        out_specs=(val, st),
        compiler_params=_params("parallel"),
        name="gla_step",
    )(q_t, k_t, a_t, v, r, state, gnorm)


def _log_decay_kernel(alow_ref, wup_ref, bias_ref, o_ref):
    o_ref[...] = _log_decay(alow_ref[...], wup_ref[...], bias_ref[...])


def log_decay(alow, w_up, bias):
    m = alow.shape[0]
    return pl.pallas_call(
        _log_decay_kernel,
        out_shape=jax.ShapeDtypeStruct((m, GLA_QK_WIDTH), F32),
        name="log_decay",
    )(alow, w_up, bias)


def _masked_softmax_steps(s_lists, mask_lists, carries, v_ts):
    c_exp = (MOBA_HEAD_DIM ** -0.5) * LOG2_E
    heads = range(len(s_lists))
    s_lists = [[jnp.where(mk, s, NEG_INF) for s, mk in zip(s_lists[e], mask_lists[e])] for e in heads]
    m_news = []
    for e in heads:
        m_new = carries[e][0]
        for s in s_lists[e]:
            m_new = jnp.maximum(m_new, jnp.max(s, axis=0, keepdims=True))
        m_news.append(m_new)
    p_alls, l_news, alphas = [], [], []
    for e in heads:
        m, l, _ = carries[e]
        p_list = [jnp.exp2((s - m_news[e]) * c_exp) for s in s_lists[e]]
        alpha = jnp.exp2((m - m_news[e]) * c_exp)
        l_new = alpha * l
        for p in p_list:
            l_new = l_new + jnp.sum(p, axis=0, keepdims=True)
        p_alls.append((p_list[0] if len(p_list) == 1 else jnp.concatenate(p_list, axis=0)).astype(BF16))
        l_news.append(l_new)
        alphas.append(alpha)
    pv = [_dot(v_ts[e], p_alls[e]) for e in heads]
    return tuple((m_news[e], l_news[e], alphas[e] * carries[e][2] + pv[e]) for e in heads)


MOBA_HEADS_PER_STEP = 4
MOBA_KV_BLOCKS_PER_ITER = 4


def _moba_prompt_kernel(q_ref, k_ref, v_ref, o_init_ref, o_ref, kb_ref, vt_ref, kmean_ref, sel_ref):
    del o_init_ref
    c = pl.program_id(2)
    blk = MOBA_BLOCK
    hd = MOBA_HEAD_DIM
    nb = k_ref.shape[0] // blk
    heads = range(MOBA_HEADS_PER_STEP)

    @pl.when(c == 0)
    def _():
        for e in heads:
            ls = slice(e * hd, (e + 1) * hd)
            for n in range(nb):
                kn = k_ref[n * blk:(n + 1) * blk, ls]
                kmean_ref[e, n:n + 1, :] = jnp.sum(kn, axis=0, keepdims=True) * (1.0 / blk)
                kb_ref[e, n * blk:(n + 1) * blk, :] = kn.astype(BF16)
                vt_ref[e, :, n * blk:(n + 1) * blk] = v_ref[n * blk:(n + 1) * blk, ls].T.astype(BF16)

    own = pl.multiple_of(c * blk, blk)
    nq = q_ref.shape[0]
    qb_ts = []
    for e in heads:
        q_t = q_ref[:, e * hd:(e + 1) * hd].T
        qb_ts.append(q_t.astype(BF16))
        gate = _dot(kmean_ref[e], q_t, precision=lax.Precision.HIGHEST)
        blk_id = lax.broadcasted_iota(jnp.int32, gate.shape, 0)
        past = blk_id < c
        for n in range(nb):
            g_n = gate[n:n + 1, :]
            beats = ((gate > g_n) | ((gate == g_n) & (blk_id < n))) & past
            rank = jnp.sum(beats.astype(F32), axis=0, keepdims=True)
            sel_ref[e, n:n + 1, :] = jnp.where((rank < MOBA_TOPK) & (n < c), 1.0, 0.0)

    causal = lax.broadcasted_iota(jnp.int32, (blk, nq), 0) <= lax.broadcasted_iota(jnp.int32, (blk, nq), 1)
    init = (jnp.full((1, nq), NEG_INF, F32), jnp.zeros((1, nq), F32), jnp.zeros((hd, nq), F32))
    carry = _masked_softmax_steps(
        [[_dot(kb_ref[e, pl.ds(own, blk), :], qb_ts[e])] for e in heads], [[causal]] * len(heads),
        [init] * len(heads), [vt_ref[e, :, pl.ds(own, blk)] for e in heads])

    g = MOBA_KV_BLOCKS_PER_ITER

    def body(j, carry):
        start = pl.multiple_of(j * (g * blk), g * blk)
        s_g = [_dot(kb_ref[e, pl.ds(start, g * blk), :], qb_ts[e]) for e in heads]
        masks = [[sel_ref[e, pl.ds(g * j + i, 1), :] > 0.0 for i in range(g)] for e in heads]
        return _masked_softmax_steps([[s[i * blk:(i + 1) * blk] for i in range(g)] for s in s_g], masks, carry,
                                     [vt_ref[e, :, pl.ds(start, g * blk)] for e in heads])

    carry = lax.fori_loop(0, (c + g - 1) // g, body, carry)
    for e in heads:
        m, l, acc = carry[e]
        o_ref[:, e * hd:(e + 1) * hd] = (acc / l).T.astype(o_ref.dtype)


def moba_prompt(proj, batch, seq, out_rows):
    nqb = seq // MOBA_BLOCK
    hp = MOBA_HEADS_PER_STEP
    hg = MOBA_HEADS // hp
    w = hp * MOBA_HEAD_DIM
    assert nqb % MOBA_KV_BLOCKS_PER_ITER == 0
    return pl.pallas_call(
        _moba_prompt_kernel,
        out_shape=jax.ShapeDtypeStruct((out_rows, MOBA_WIDTH), BF16),
        grid=(batch, hg, nqb),
        in_specs=[
            pl.BlockSpec((MOBA_BLOCK, w), lambda b, h, c: (b * nqb + c, h)),
            pl.BlockSpec((seq, w), lambda b, h, c: (b, hg + h)),
            pl.BlockSpec((seq, w), lambda b, h, c: (b, 2 * hg + h)),
            pl.BlockSpec(memory_space=pl.ANY),
        ],
        out_specs=pl.BlockSpec((MOBA_BLOCK, w), lambda b, h, c: (b * nqb + c, h)),
        scratch_shapes=[pltpu.VMEM((hp, seq, MOBA_HEAD_DIM), BF16), pltpu.VMEM((hp, MOBA_HEAD_DIM, seq), BF16),
                        pltpu.VMEM((hp, nqb, MOBA_HEAD_DIM), F32), pltpu.VMEM((hp, nqb, MOBA_BLOCK), F32)],
        input_output_aliases={3: 0},
        compiler_params=_params("parallel", "parallel", "arbitrary"),
        name="moba_prompt",
    )(proj, proj, proj, jnp.zeros((out_rows, MOBA_WIDTH), BF16))


MOBA_STEP_PAGES = 8


def _moba_step_kernel(pt_ref, q_ref, kn_ref, vn_ref, *refs):
    npg = MOBA_STEP_PAGES
    k_refs, cache_v = refs[:npg], refs[npg]
    o_ref, p_buf, m_buf, l_buf, g_buf, own_buf, sel_buf, v_buf, sel_smem, sem = refs[npg + 1:]
    b, j = pl.program_id(0), pl.program_id(1)
    n_seq, n_steps = pl.num_programs(0) - 1, pl.num_programs(1)
    hn, hd = MOBA_HEADS, MOBA_HEAD_DIM
    page = k_refs[0].shape[1]
    n_blocks = m_buf.shape[1]
    n_pages = 2 * n_blocks
    scale = hd ** -0.5
    lanes = page * hn
    cur = b % 2
    prev = 1 - cur

    def v_copy(par, seq, s, h, pg):
        blk = sel_smem[par, s * hn + h]
        page_id = pt_ref[seq * n_pages + 2 * blk + pg]
        return pltpu.make_async_copy(cache_v.at[page_id, :, h, :],
                                     v_buf.at[par, s, pl.ds(pg * page, page), h, :], sem.at[par])

    slots = [(s, h, pg) for s in range(MOBA_TOPK) for h in range(hn) for pg in range(2)]

    @pl.when((b >= 1) & (j == n_steps - 1))
    def _finish():
        for s, h, pg in slots:
            v_copy(prev, b - 1, s, h, pg).wait()
        s_own = own_buf[prev]
        m_sel, l_sel, pv = [], [], []
        for s in range(MOBA_TOPK):
            sel = sel_buf[prev, s]
            picks = [sel == float(n) for n in range(n_blocks)]
            m_sel.append(sum(jnp.where(pk, m_buf[prev, n], 0.0) for n, pk in enumerate(picks)))
            l_sel.append(sum(jnp.where(pk, l_buf[prev, n], 0.0) for n, pk in enumerate(picks)))
            p_s = jnp.zeros((hn, 2 * lanes), BF16)
            for n, pk in enumerate(picks):
                p_s = jnp.where(pk, p_buf[prev, n], p_s)
            pv.append(_dot(p_s, v_buf[prev, s].reshape(2 * lanes, hd).astype(BF16)))
        m_all = s_own
        for m_s in m_sel:
            m_all = jnp.maximum(m_all, m_s)
        w_own = jnp.exp(s_own - m_all)
        den = w_own
        num = w_own * vn_ref[0]
        for m_s, l_s, pv_s in zip(m_sel, l_sel, pv):
            w = jnp.exp(m_s - m_all)
            den += w * l_s
            num += w * pv_s
        o_ref[0] = (num / den).astype(o_ref.dtype)

    @pl.when(b < n_seq)
    def _score():
        q = q_ref[0]
        qb = q.astype(BF16)
        same_head = ((lax.broadcasted_iota(jnp.int32, (hn, lanes), 1) % hn)
                     == lax.broadcasted_iota(jnp.int32, (hn, lanes), 0))
        blocks = range(npg // 2)
        s_parts = [_dot_nt(qb, k_ref[0].reshape(lanes, hd).astype(BF16)) for k_ref in k_refs]
        gsum = [jnp.sum(jnp.where(same_head, s, 0.0), axis=-1, keepdims=True) for s in s_parts]
        s_m = [jnp.where(same_head, s * scale, NEG_INF) for s in s_parts]
        s_max = [jnp.max(s, axis=-1, keepdims=True) for s in s_m]
        m = [jnp.maximum(s_max[2 * i], s_max[2 * i + 1]) for i in blocks]
        p = [jnp.exp(s - m[i // 2]) for i, s in enumerate(s_m)]
        p_sum = [jnp.sum(p_i, axis=-1, keepdims=True) for p_i in p]
        for i in blocks:
            n = j * (npg // 2) + i
            g_buf[cur, n] = gsum[2 * i] + gsum[2 * i + 1]
            m_buf[cur, n] = m[i]
            l_buf[cur, n] = p_sum[2 * i] + p_sum[2 * i + 1]
            p_buf[cur, n] = jnp.concatenate([p[2 * i], p[2 * i + 1]], axis=1).astype(BF16)

        @pl.when(j == n_steps - 1)
        def _select():
            own_buf[cur] = jnp.sum(q * kn_ref[0], axis=-1, keepdims=True) * scale
            gates = [g_buf[cur, n] for n in range(n_blocks)]
            ranks = []
            for n in range(n_blocks):
                rank = jnp.zeros((hn, 1), F32)
                for n2 in range(n_blocks):
                    if n2 != n:
                        beats = (gates[n2] > gates[n]) | ((gates[n2] == gates[n]) & (n2 < n))
                        rank += beats.astype(F32)
                ranks.append(rank)
            head = lax.broadcasted_iota(jnp.int32, (hn, 1), 0)
            for s in range(MOBA_TOPK):
                sel = sum(jnp.where(ranks[n] == float(s), float(n), 0.0) for n in range(n_blocks))
                sel_buf[cur, s] = sel
                for h in range(hn):
                    sel_smem[cur, s * hn + h] = jnp.sum(jnp.where(head == h, sel, 0.0)).astype(jnp.int32)
            for s, h, pg in slots:
                v_copy(cur, b, s, h, pg).start()


def moba_step(q, k_new, v_new, cache_k, cache_v, page_table):
    nb, n_pages = page_table.shape
    page = cache_k.shape[1]
    npg = MOBA_STEP_PAGES
    n_blocks = n_pages // 2
    assert MOBA_BLOCK == 2 * page and n_pages % npg == 0 and n_blocks >= MOBA_TOPK
    hn, hd = MOBA_HEADS, MOBA_HEAD_DIM
    scored = pl.BlockSpec((1, hn, hd), lambda b, j, pt: (jnp.minimum(b, nb - 1), 0, 0))
    merged = pl.BlockSpec((1, hn, hd), lambda b, j, pt: (jnp.maximum(b - 1, 0), 0, 0))

    def page_spec(i):
        return pl.BlockSpec((1, page, hn, hd),
                            lambda b, j, pt: (pt[jnp.minimum(b, nb - 1) * n_pages + j * npg + i], 0, 0, 0))

    return pl.pallas_call(
        _moba_step_kernel,
        out_shape=jax.ShapeDtypeStruct((nb, hn, hd), BF16),
        grid_spec=pltpu.PrefetchScalarGridSpec(
            num_scalar_prefetch=1,
            grid=(nb + 1, n_pages // npg),
            in_specs=[scored, scored, merged] + [page_spec(i) for i in range(npg)]
                     + [pl.BlockSpec(memory_space=pl.ANY)],
            out_specs=merged,
            scratch_shapes=[
                pltpu.VMEM((2, n_blocks, hn, 2 * page * hn), BF16),
                pltpu.VMEM((2, n_blocks, hn, 1), F32),
                pltpu.VMEM((2, n_blocks, hn, 1), F32),
                pltpu.VMEM((2, n_blocks, hn, 1), F32),
                pltpu.VMEM((2, hn, 1), F32),
                pltpu.VMEM((2, MOBA_TOPK, hn, 1), F32),
                pltpu.VMEM((2, MOBA_TOPK, 2 * page, hn, hd), F32),
                pltpu.SMEM((2, MOBA_TOPK * hn), jnp.int32),
                pltpu.SemaphoreType.DMA((2,)),
            ],
        ),
        compiler_params=_params("arbitrary", "arbitrary"),
        name="moba_step",
    )(page_table.reshape(-1), q, k_new, v_new, *([cache_k] * npg), cache_v)


def _row_tile(m, target):
    best = None
    for t in range(16, target + 1, 16):
        if m % t == 0:
            best = t
    assert best is not None
    return best


def kernel(x_prompt, x_sample, cache_k, cache_v, state_gla, page_table, norm_mix_pre, norm_mix_post, w_in,
           w_gla_gate_up, b_gla_gate, gla_norm, w_merge, w_branch_gla, w_branch_moba, w_out, norm_ffn_pre,
           norm_ffn_post, w_ffn_gate, w_ffn_up, w_ffn_down):
    batch, seq, d = x_prompt.shape
    nb_s = x_sample.shape[0]
    mp = batch * seq
    m_all = mp + nb_s
    gla_w = 2 * GLA_QK_WIDTH + 2 * GLA_V_WIDTH
    moba0 = gla_w + GLA_GATE_RANK
    hn, hd = MOBA_HEADS, MOBA_HEAD_DIM

    w_in_t = w_in.T
    w_down = w_ffn_down.astype(BF16)
    w_up_pad = jnp.pad(w_gla_gate_up, ((0, LANE - GLA_GATE_RANK), (0, 0)))
    bias = b_gla_gate.reshape(1, -1)
    gnorm = gla_norm.reshape(1, -1)

    xp = x_prompt.reshape(mp, d)
    xs = x_sample.reshape(nb_s, d)
    tm = _row_tile(m_all, 1040)
    xn = rmsnorm_groups(xp, xs, norm_mix_pre)
    p_gla, p_low = matmul_nt(xn, w_in_t, tm, 512, 0, gla_w, side_row0=gla_w)
    p_moba = matmul_nt(xn, w_in_t, tm, 512, moba0, 3 * MOBA_WIDTH)

    o_g, state_p = gla_prompt(p_gla, p_low, w_up_pad, bias, gnorm, batch, seq, m_all)
    o_m = moba_prompt(p_moba, batch, seq, m_all)

    s_gla, s_moba = p_gla[mp:], p_moba[mp:]
    log_a = log_decay(p_low[mp:], w_up_pad, bias)

    def cols(t):
        return t.reshape(nb_s, GLA_HEADS, GLA_DK).transpose(0, 2, 1)

    o_gs, state_s = gla_step(
        cols(s_gla[:, :GLA_QK_WIDTH]), cols(s_gla[:, GLA_QK_WIDTH:2 * GLA_QK_WIDTH]), cols(log_a),
        s_gla[:, 2 * GLA_QK_WIDTH:2 * GLA_QK_WIDTH + GLA_V_WIDTH].reshape(nb_s, GLA_HEADS, GLA_DV),
        s_gla[:, 2 * GLA_QK_WIDTH + GLA_V_WIDTH:].reshape(nb_s, GLA_HEADS, GLA_DV),
        state_gla, gnorm, 4)
    q_s = s_moba[:, :MOBA_WIDTH].reshape(nb_s, hn, hd)
    k_s = s_moba[:, MOBA_WIDTH:2 * MOBA_WIDTH].reshape(nb_s, hn, hd)
    v_s = s_moba[:, 2 * MOBA_WIDTH:].reshape(nb_s, hn, hd)
    o_ms = moba_step(q_s, k_s, v_s, cache_k, cache_v, page_table)
    o_g = lax.dynamic_update_slice(o_g, o_gs.reshape(nb_s, GLA_V_WIDTH), (mp, 0))
    o_m = lax.dynamic_update_slice(o_m, o_ms.reshape(nb_s, MOBA_WIDTH), (mp, 0))

    mixed = gated_mix(xn, o_g, o_m, w_merge, w_branch_gla, w_branch_moba, tm, 256)
    h, h_n = norm_residual_groups(matmul(mixed, w_out, tm, 512), xp, xs, norm_mix_post, norm_ffn_pre)
    a = swiglu_up(h_n, w_ffn_gate, w_ffn_up, tm, 256)
    f = matmul(a, w_down, _row_tile(m_all, 520), 512)
    y_p = norm_residual_rows(f, h, norm_ffn_post, 0, mp)
    y_s = norm_residual_rows(f, h, norm_ffn_post, mp, nb_s)

    k_p = p_moba[:mp, MOBA_WIDTH:2 * MOBA_WIDTH].reshape(batch, seq, hn, hd)
    v_p = p_moba[:mp, 2 * MOBA_WIDTH:].reshape(batch, seq, hn, hd)
    return (y_p.reshape(batch, seq, d), y_s.reshape(nb_s, 1, d), k_p, v_p, state_p,
            k_s.reshape(nb_s, 1, hn, hd), v_s.reshape(nb_s, 1, hn, hd), state_s)
```

```python
import math

import jax
import jax.numpy as jnp
from jax import lax
from jax.experimental import pallas as pl
from jax.experimental.pallas import tpu as pltpu

F32 = jnp.float32
BF16 = jnp.bfloat16

GLA_HEADS = 8
GLA_DK = 128
GLA_DV = 256
GLA_QK_WIDTH = GLA_HEADS * GLA_DK
GLA_V_WIDTH = GLA_HEADS * GLA_DV
GLA_GATE_RANK = 16
GLA_GATE_TAU = 16.0
GLA_CHUNK = 64
GLA_SUB = 16
MOBA_HEADS = 16
MOBA_HEAD_DIM = 128
MOBA_WIDTH = MOBA_HEADS * MOBA_HEAD_DIM
MOBA_BLOCK = 256
MOBA_TOPK = 3
RMS_EPS = 1e-6
LANE = 128
VMEM_LIMIT = 56 * 1024 * 1024
NEG_INF = float("-inf")
LOG2_E = math.log2(math.e)
ROW_TILE = 128


def _params(*sem):
    return pltpu.CompilerParams(dimension_semantics=sem, vmem_limit_bytes=VMEM_LIMIT)


def _sigmoid(x):
    return 1.0 / (1.0 + jnp.exp(-x))


def _log_sigmoid(x):
    return jnp.minimum(x, 0.0) - jnp.log1p(jnp.exp(-jnp.abs(x)))


def _dot(a, b, precision=None):
    return jnp.dot(a, b, precision=precision, preferred_element_type=F32)


def _dot_nt(a, b):
    return lax.dot_general(a, b, (((1,), (1,)), ((), ())), preferred_element_type=F32)


def _dot_tn(a, b):
    return lax.dot_general(a, b, (((0,), (0,)), ((), ())), preferred_element_type=F32)


def _bf16(w):
    return w if w.dtype == BF16 else w.astype(BF16)


def _rms(x, w):
    return x * lax.rsqrt(jnp.mean(x * x, axis=-1, keepdims=True) + RMS_EPS) * w


def _group_specs(mp, ms, d):
    npb = mp // ROW_TILE
    assert mp % ROW_TILE == 0 and ms % ROW_TILE == 0
    p_spec = pl.BlockSpec((ROW_TILE, d), lambda i: (jnp.minimum(i, npb - 1), 0))
    s_spec = pl.BlockSpec((ROW_TILE, d), lambda i: (jnp.maximum(i - npb, 0), 0))
    return npb, p_spec, s_spec


def rmsnorm_groups(x_p, x_s, w):
    (mp, d), ms = x_p.shape, x_s.shape[0]
    npb, p_spec, s_spec = _group_specs(mp, ms, d)

    def kern(xp_ref, xs_ref, w_ref, o_ref):
        i = pl.program_id(0)

        @pl.when(i < npb)
        def _():
            o_ref[...] = _rms(xp_ref[...], w_ref[...]).astype(o_ref.dtype)

        @pl.when(i >= npb)
        def _():
            o_ref[...] = _rms(xs_ref[...], w_ref[...]).astype(o_ref.dtype)

    return pl.pallas_call(
        kern,
        out_shape=jax.ShapeDtypeStruct((mp + ms, d), BF16),
        grid=((mp + ms) // ROW_TILE,),
        in_specs=[p_spec, s_spec, pl.BlockSpec((1, d), lambda i: (0, 0))],
        out_specs=pl.BlockSpec((ROW_TILE, d), lambda i: (i, 0)),
        compiler_params=_params("arbitrary"),
        name="rmsnorm_groups",
    )(x_p, x_s, w.reshape(1, d))


def norm_residual_groups(f, x_p, x_s, w, w_next):
    (mp, d), ms = x_p.shape, x_s.shape[0]
    npb, p_spec, s_spec = _group_specs(mp, ms, d)
    row = pl.BlockSpec((ROW_TILE, d), lambda i: (i, 0))
    vec = pl.BlockSpec((1, d), lambda i: (0, 0))

    def kern(f_ref, xp_ref, xs_ref, w_ref, wn_ref, h_ref, hn_ref):
        i = pl.program_id(0)

        def body(x_ref):
            h = x_ref[...] + _rms(f_ref[...], w_ref[...])
            h_ref[...] = h
            hn_ref[...] = _rms(h, wn_ref[...]).astype(hn_ref.dtype)

        pl.when(i < npb)(lambda: body(xp_ref))
        pl.when(i >= npb)(lambda: body(xs_ref))

    return pl.pallas_call(
        kern,
        out_shape=(jax.ShapeDtypeStruct((mp + ms, d), F32), jax.ShapeDtypeStruct((mp + ms, d), BF16)),
        grid=((mp + ms) // ROW_TILE,),
        in_specs=[row, p_spec, s_spec, vec, vec],
        out_specs=(row, row),
        compiler_params=_params("arbitrary"),
        name="norm_residual_groups",
    )(f, x_p, x_s, w.reshape(1, d), w_next.reshape(1, d))


def norm_residual_rows(f, h, w, row0, rows):
    d = h.shape[1]
    tile = math.gcd(math.gcd(row0, rows), 2 * ROW_TILE)
    b0 = row0 // tile
    assert tile % 8 == 0

    def kern(f_ref, h_ref, w_ref, o_ref):
        o_ref[...] = h_ref[...] + _rms(f_ref[...], w_ref[...])

    src = pl.BlockSpec((tile, d), lambda i: (i + b0, 0))
    return pl.pallas_call(
        kern,
        out_shape=jax.ShapeDtypeStruct((rows, d), F32),
        grid=(rows // tile,),
        in_specs=[src, src, pl.BlockSpec((1, d), lambda i: (0, 0))],
        out_specs=pl.BlockSpec((tile, d), lambda i: (i, 0)),
        compiler_params=_params("arbitrary"),
        name="norm_residual_rows",
    )(f, h, w.reshape(1, d))


def _mm_kernel(x_ref, w_ref, o_ref):
    o_ref[...] = _dot(x_ref[...], _bf16(w_ref[...])).astype(o_ref.dtype)


def matmul(x, w, tm, tn, n_cols=None, col0=0, out_dtype=F32):
    m, k = x.shape
    n = w.shape[1] if n_cols is None else n_cols
    assert m % tm == 0
    return pl.pallas_call(
        _mm_kernel,
        out_shape=jax.ShapeDtypeStruct((m, n), out_dtype),
        grid=(m // tm, pl.cdiv(n, tn)),
        in_specs=[pl.BlockSpec((tm, k), lambda i, j: (i, 0)), pl.BlockSpec((k, tn), lambda i, j: (0, j + col0))],
        out_specs=pl.BlockSpec((tm, tn), lambda i, j: (i, j)),
        compiler_params=_params("parallel", "arbitrary"),
        name="matmul",
    )(x, w)


def _mm_nt_kernel(x_ref, w_ref, o_ref):
    o_ref[...] = _dot_nt(x_ref[...], _bf16(w_ref[...])).astype(o_ref.dtype)


def matmul_nt(x, w_t, tm, tn, row0, n_cols):
    m, k = x.shape
    assert m % tm == 0 and n_cols % tn == 0 and row0 % 8 == 0 and row0 + n_cols <= w_t.shape[0]
    return pl.pallas_call(
        _mm_nt_kernel,
        out_shape=jax.ShapeDtypeStruct((m, n_cols), F32),
        grid=(m // tm, n_cols // tn),
        in_specs=[pl.BlockSpec((tm, k), lambda i, j: (i, 0)),
                  pl.BlockSpec((pl.Element(tn), pl.Element(k)), lambda i, j: ((row0 // 8 + j * (tn // 8)) * 8, 0))],
        out_specs=pl.BlockSpec((tm, tn), lambda i, j: (i, j)),
        compiler_params=_params("parallel", "arbitrary"),
        name="matmul_nt",
    )(x, w_t)


def _mix_kernel(xn_ref, og_ref, om_ref, wga_ref, wgb_ref, wba_ref, wbb_ref, o_ref):
    xn = xn_ref[...]
    g_a = _sigmoid(_dot(xn, _bf16(wga_ref[...])))
    g_b = _sigmoid(_dot(xn, _bf16(wgb_ref[...])))
    mixed = g_a * _dot(og_ref[...], _bf16(wba_ref[...])) + g_b * _dot(om_ref[...], _bf16(wbb_ref[...]))
    o_ref[...] = mixed.astype(o_ref.dtype)


def gated_mix(xn, o_g, o_m, w_merge, w_bg, w_bm, tm, tn):
    m, d = xn.shape
    nb = d // tn
    assert m % tm == 0 and d % tn == 0

    def resident(width):
        return pl.BlockSpec((tm, width), lambda i, j: (i, 0), pipeline_mode=pl.Buffered(1))

    return pl.pallas_call(
        _mix_kernel,
        out_shape=jax.ShapeDtypeStruct((m, d), BF16),
        grid=(m // tm, nb),
        in_specs=[
            resident(d), resident(o_g.shape[1]), resident(o_m.shape[1]),
            pl.BlockSpec((d, tn), lambda i, j: (0, j)),
            pl.BlockSpec((d, tn), lambda i, j: (0, j + nb)),
            pl.BlockSpec((w_bg.shape[0], tn), lambda i, j: (0, j)),
            pl.BlockSpec((w_bm.shape[0], tn), lambda i, j: (0, j)),
        ],
        out_specs=pl.BlockSpec((tm, tn), lambda i, j: (i, j)),
        compiler_params=_params("parallel", "arbitrary"),
        name="gated_mix",
    )(xn, o_g, o_m, w_merge, w_merge, w_bg, w_bm)


def _swiglu_kernel(x_ref, wg_ref, wu_ref, o_ref):
    x = x_ref[...]
    g = _dot(x, _bf16(wg_ref[...]))
    u = _dot(x, _bf16(wu_ref[...]))
    o_ref[...] = (g * _sigmoid(g) * u).astype(o_ref.dtype)


def swiglu_up(x, w_gate, w_up, tm, tn):
    m, k = x.shape
    n = w_gate.shape[1]
    assert m % tm == 0 and n % tn == 0
    return pl.pallas_call(
        _swiglu_kernel,
        out_shape=jax.ShapeDtypeStruct((m, n), BF16),
        grid=(m // tm, n // tn),
        in_specs=[pl.BlockSpec((tm, k), lambda i, j: (i, 0)),
                  pl.BlockSpec((k, tn), lambda i, j: (0, j)),
                  pl.BlockSpec((k, tn), lambda i, j: (0, j))],
        out_specs=pl.BlockSpec((tm, tn), lambda i, j: (i, j)),
        compiler_params=_params("parallel", "arbitrary"),
        name="swiglu_up",
    )(x, w_gate, w_up)


def _gla_head(q, k, v, b, s_t):
    c = q.shape[0]
    nsub = c // GLA_SUB
    row = lax.broadcasted_iota(jnp.int32, (c, c), 0)
    col = lax.broadcasted_iota(jnp.int32, (c, c), 1)
    diff = row - col
    row_blk = row // GLA_SUB
    same_sub = (row_blk == col // GLA_SUB) & (col <= row)

    o_inter = _dot_nt((q * jnp.exp(b)).astype(BF16), s_t.astype(BF16))

    grp = 8
    assert GLA_SUB == 2 * grp and c % GLA_SUB == 0
    dk = q.shape[1]
    tiles = lambda x: x.reshape(c // grp, grp, dk)
    rot = lambda x, r: x if r == 0 else pltpu.roll(x, r, 1)
    q3, k3, b3 = tiles(q), tiles(k), tiles(b)
    a_same = jnp.zeros((c, c), F32)
    for r in range(grp):
        w = q3 * rot(k3, r) * jnp.exp(jnp.minimum(b3 - rot(b3, r), 0.0))
        a_same = jnp.where(diff == r, jnp.sum(w, axis=-1, keepdims=True).reshape(c, 1), a_same)
    halves = lambda x: x.reshape(nsub, 2, grp, dk)
    q_hi, k_lo, b_hi, b_lo = halves(q)[:, 1], halves(k)[:, 0], halves(b)[:, 1], halves(b)[:, 0]
    a_cross = jnp.zeros((c, c), F32)
    for r in range(grp):
        w = q_hi * rot(k_lo, r) * jnp.exp(b_hi - rot(b_lo, r))
        val = jnp.sum(w, axis=-1, keepdims=True)
        val = jnp.broadcast_to(val[:, None], (nsub, 2, grp, 1)).reshape(c, 1)
        a_cross = jnp.where((diff & (grp - 1)) == r, val, a_cross)
    same_grp = (row // grp == col // grp) & (col <= row)
    a_band = jnp.where(same_grp, a_same, a_cross)

    if nsub > 1:
        t_row = lax.broadcasted_iota(jnp.int32, (c, 1), 0)
        t_blk = t_row // GLA_SUB
        q_parts, k_parts = [], []
        for i in range(1, nsub):
            r_i = b[i * GLA_SUB - 1:i * GLA_SUB, :]
            q_i = jnp.where(t_blk == i, q * jnp.exp(jnp.minimum(b - r_i, 0.0)), 0.0)
            k_i = jnp.where(t_row < i * GLA_SUB, k * jnp.exp(jnp.minimum(r_i - b, 0.0)), 0.0)
            q_parts.append(q_i.astype(BF16))
            k_parts.append(k_i.astype(BF16))
        a_off = _dot_nt(jnp.concatenate(q_parts, axis=1), jnp.concatenate(k_parts, axis=1))
        a = jnp.where(same_sub, a_band, jnp.where(col < row_blk * GLA_SUB, a_off, 0.0))
    else:
        a = jnp.where(same_sub, a_band, 0.0)

    o = o_inter + _dot(a.astype(BF16), v.astype(BF16))
    b_last = b[c - 1:c, :]
    k_dec = k * jnp.exp(b_last - b)
    s_new = s_t * jnp.exp(b_last) + _dot_tn(v.astype(BF16), k_dec.astype(BF16))
    return o, s_new


def _gla_out(o, r, gnorm):
    return _rms(o, gnorm) * (r * _sigmoid(r))


def _log_decay(alow, wup, bias):
    z = _dot(alow.astype(BF16), wup.astype(BF16)) + bias
    return _log_sigmoid(z) / GLA_GATE_TAU


def _gla_prompt_kernel(q_ref, k_ref, v_ref, r_ref, alow_ref, wup_ref, bias_ref, gnorm_ref, o_init_ref,
                       o_ref, state_ref, st_ref):
    del o_init_ref
    t = pl.program_id(1)
    c = q_ref.shape[0]

    @pl.when(t == 0)
    def _():
        st_ref[...] = jnp.zeros_like(st_ref)

    log_a = _log_decay(alow_ref[...], wup_ref[...], bias_ref[...])
    tril = (lax.broadcasted_iota(jnp.int32, (c, c), 1) <= lax.broadcasted_iota(jnp.int32, (c, c), 0)).astype(F32)
    b_all = _dot(tril, log_a, precision=lax.Precision.HIGHEST)
    gnorm = gnorm_ref[...]
    scale = GLA_DK ** -0.5
    for h in range(GLA_HEADS):
        ks = slice(h * GLA_DK, (h + 1) * GLA_DK)
        vs = slice(h * GLA_DV, (h + 1) * GLA_DV)
        o, s_new = _gla_head(q_ref[:, ks] * scale, k_ref[:, ks], v_ref[:, vs], b_all[:, ks], st_ref[h])
        st_ref[h] = s_new
        o_ref[:, vs] = _gla_out(o, r_ref[:, vs], gnorm).astype(o_ref.dtype)

    @pl.when(t == pl.num_programs(1) - 1)
    def _():
        for h in range(GLA_HEADS):
            state_ref[0, h] = st_ref[h].T


def gla_prompt(proj, alow, w_up, bias, gnorm, batch, seq, out_rows):
    n_chunks = seq // GLA_CHUNK
    c = GLA_CHUNK

    def rows(b, t):
        return b * n_chunks + t

    return pl.pallas_call(
        _gla_prompt_kernel,
        out_shape=(jax.ShapeDtypeStruct((out_rows, GLA_V_WIDTH), BF16),
                   jax.ShapeDtypeStruct((batch, GLA_HEADS, GLA_DK, GLA_DV), F32)),
        grid=(batch, n_chunks),
        in_specs=[
            pl.BlockSpec((c, GLA_QK_WIDTH), lambda b, t: (rows(b, t), 0)),
            pl.BlockSpec((c, GLA_QK_WIDTH), lambda b, t: (rows(b, t), 1)),
            pl.BlockSpec((c, GLA_V_WIDTH), lambda b, t: (rows(b, t), 1)),
            pl.BlockSpec((c, GLA_V_WIDTH), lambda b, t: (rows(b, t), 2)),
            pl.BlockSpec((c, LANE), lambda b, t: (rows(b, t), 0)),
            pl.BlockSpec((LANE, GLA_QK_WIDTH), lambda b, t: (0, 0)),
            pl.BlockSpec((1, GLA_QK_WIDTH), lambda b, t: (0, 0)),
            pl.BlockSpec((1, GLA_DV), lambda b, t: (0, 0)),
            pl.BlockSpec(memory_space=pl.ANY),
        ],
        out_specs=(pl.BlockSpec((c, GLA_V_WIDTH), lambda b, t: (rows(b, t), 0)),
                   pl.BlockSpec((1, GLA_HEADS, GLA_DK, GLA_DV), lambda b, t: (b, 0, 0, 0))),
        scratch_shapes=[pltpu.VMEM((GLA_HEADS, GLA_DV, GLA_DK), F32)],
        input_output_aliases={8: 0},
        compiler_params=_params("parallel", "arbitrary"),
        name="gla_prompt",
    )(proj, proj, proj, proj, alow, w_up, bias, gnorm, jnp.zeros((out_rows, GLA_V_WIDTH), BF16))


def _gla_step_kernel(qt_ref, kt_ref, at_ref, v_ref, r_ref, s_ref, gnorm_ref, o_ref, snew_ref):
    bb = s_ref.shape[0]
    scale = GLA_DK ** -0.5
    gnorm = gnorm_ref[...]
    for i in range(bb):
        q_t = qt_ref[i] * scale
        k_t = kt_ref[i]
        a_t = jnp.exp(at_ref[i])
        v = v_ref[i]
        r = r_ref[i]
        outs = []
        for h in range(GLA_HEADS):
            s_new = s_ref[i, h] * a_t[:, h:h + 1] + k_t[:, h:h + 1] * v[h:h + 1, :]
            snew_ref[i, h] = s_new
            outs.append(jnp.sum(q_t[:, h:h + 1] * s_new, axis=0, keepdims=True))
        o = jnp.concatenate(outs, axis=0)
        o_ref[i] = _gla_out(o, r, gnorm).astype(o_ref.dtype)


def gla_step(q_t, k_t, a_t, v, r, state, gnorm, bb):
    nb = state.shape[0]
    col = pl.BlockSpec((bb, GLA_DK, GLA_HEADS), lambda i: (i, 0, 0))
    val = pl.BlockSpec((bb, GLA_HEADS, GLA_DV), lambda i: (i, 0, 0))
    st = pl.BlockSpec((bb, GLA_HEADS, GLA_DK, GLA_DV), lambda i: (i, 0, 0, 0))
    return pl.pallas_call(
        _gla_step_kernel,
        out_shape=(jax.ShapeDtypeStruct((nb, GLA_HEADS, GLA_DV), BF16),
                   jax.ShapeDtypeStruct(state.shape, F32)),
        grid=(nb // bb,),
        in_specs=[col, col, col, val, val, st, pl.BlockSpec((1, GLA_DV), lambda i: (0, 0))],
        out_specs=(val, st),
        compiler_params=_params("parallel"),
        name="gla_step",
    )(q_t, k_t, a_t, v, r, state, gnorm)


def _log_decay_kernel(alow_ref, wup_ref, bias_ref, o_ref):
    o_ref[...] = _log_decay(alow_ref[...], wup_ref[...], bias_ref[...])


def log_decay(alow, w_up, bias):
    m = alow.shape[0]
    return pl.pallas_call(
        _log_decay_kernel,
        out_shape=jax.ShapeDtypeStruct((m, GLA_QK_WIDTH), F32),
        name="log_decay",
    )(alow, w_up, bias)


def _masked_softmax_steps(s_lists, mask_lists, carries, v_ts):
    c_exp = (MOBA_HEAD_DIM ** -0.5) * LOG2_E
    heads = range(len(s_lists))
    s_lists = [[jnp.where(mk, s, NEG_INF) for s, mk in zip(s_lists[e], mask_lists[e])] for e in heads]
    m_news = []
    for e in heads:
        m_new = carries[e][0]
        for s in s_lists[e]:
            m_new = jnp.maximum(m_new, jnp.max(s, axis=0, keepdims=True))
        m_news.append(m_new)
    p_alls, l_news, alphas = [], [], []
    for e in heads:
        m, l, _ = carries[e]
        p_list = [jnp.exp2((s - m_news[e]) * c_exp) for s in s_lists[e]]
        alpha = jnp.exp2((m - m_news[e]) * c_exp)
        l_new = alpha * l
        for p in p_list:
            l_new = l_new + jnp.sum(p, axis=0, keepdims=True)
        p_alls.append((p_list[0] if len(p_list) == 1 else jnp.concatenate(p_list, axis=0)).astype(BF16))
        l_news.append(l_new)
        alphas.append(alpha)
    pv = [_dot(v_ts[e], p_alls[e]) for e in heads]
    return tuple((m_news[e], l_news[e], alphas[e] * carries[e][2] + pv[e]) for e in heads)


MOBA_HEADS_PER_STEP = 4
MOBA_KV_BLOCKS_PER_ITER = 4


def _moba_prompt_kernel(q_ref, k_ref, v_ref, o_init_ref, o_ref, kb_ref, vt_ref, kmean_ref, sel_ref):
    del o_init_ref
    c = pl.program_id(2)
    blk = MOBA_BLOCK
    hd = MOBA_HEAD_DIM
    nb = k_ref.shape[0] // blk
    heads = range(MOBA_HEADS_PER_STEP)

    @pl.when(c == 0)
    def _():
        for e in heads:
            ls = slice(e * hd, (e + 1) * hd)
            for n in range(nb):
                kn = k_ref[n * blk:(n + 1) * blk, ls]
                kmean_ref[e, n:n + 1, :] = jnp.sum(kn, axis=0, keepdims=True) * (1.0 / blk)
                kb_ref[e, n * blk:(n + 1) * blk, :] = kn.astype(BF16)
                vt_ref[e, :, n * blk:(n + 1) * blk] = v_ref[n * blk:(n + 1) * blk, ls].T.astype(BF16)

    own = pl.multiple_of(c * blk, blk)
    nq = q_ref.shape[0]
    qb_ts = []
    for e in heads:
        q_t = q_ref[:, e * hd:(e + 1) * hd].T
        qb_ts.append(q_t.astype(BF16))
        gate = _dot(kmean_ref[e], q_t, precision=lax.Precision.HIGHEST)
        blk_id = lax.broadcasted_iota(jnp.int32, gate.shape, 0)
        past = blk_id < c
        for n in range(nb):
            g_n = gate[n:n + 1, :]
            beats = ((gate > g_n) | ((gate == g_n) & (blk_id < n))) & past
            rank = jnp.sum(beats.astype(F32), axis=0, keepdims=True)
            sel_ref[e, n:n + 1, :] = jnp.where((rank < MOBA_TOPK) & (n < c), 1.0, 0.0)

    causal = lax.broadcasted_iota(jnp.int32, (blk, nq), 0) <= lax.broadcasted_iota(jnp.int32, (blk, nq), 1)
    init = (jnp.full((1, nq), NEG_INF, F32), jnp.zeros((1, nq), F32), jnp.zeros((hd, nq), F32))
    carry = _masked_softmax_steps(
        [[_dot(kb_ref[e, pl.ds(own, blk), :], qb_ts[e])] for e in heads], [[causal]] * len(heads),
        [init] * len(heads), [vt_ref[e, :, pl.ds(own, blk)] for e in heads])

    g = MOBA_KV_BLOCKS_PER_ITER

    def body(j, carry):
        start = pl.multiple_of(j * (g * blk), g * blk)
        s_g = [_dot(kb_ref[e, pl.ds(start, g * blk), :], qb_ts[e]) for e in heads]
        masks = [[sel_ref[e, pl.ds(g * j + i, 1), :] > 0.0 for i in range(g)] for e in heads]
        return _masked_softmax_steps([[s[i * blk:(i + 1) * blk] for i in range(g)] for s in s_g], masks, carry,
                                     [vt_ref[e, :, pl.ds(start, g * blk)] for e in heads])

    carry = lax.fori_loop(0, (c + g - 1) // g, body, carry)
    for e in heads:
        m, l, acc = carry[e]
        o_ref[:, e * hd:(e + 1) * hd] = (acc / l).T.astype(o_ref.dtype)


def moba_prompt(proj, batch, seq, out_rows):
    nqb = seq // MOBA_BLOCK
    hp = MOBA_HEADS_PER_STEP
    hg = MOBA_HEADS // hp
    w = hp * MOBA_HEAD_DIM
    assert nqb % MOBA_KV_BLOCKS_PER_ITER == 0
    return pl.pallas_call(
        _moba_prompt_kernel,
        out_shape=jax.ShapeDtypeStruct((out_rows, MOBA_WIDTH), BF16),
        grid=(batch, hg, nqb),
        in_specs=[
            pl.BlockSpec((MOBA_BLOCK, w), lambda b, h, c: (b * nqb + c, h)),
            pl.BlockSpec((seq, w), lambda b, h, c: (b, hg + h)),
            pl.BlockSpec((seq, w), lambda b, h, c: (b, 2 * hg + h)),
            pl.BlockSpec(memory_space=pl.ANY),
        ],
        out_specs=pl.BlockSpec((MOBA_BLOCK, w), lambda b, h, c: (b * nqb + c, h)),
        scratch_shapes=[pltpu.VMEM((hp, seq, MOBA_HEAD_DIM), BF16), pltpu.VMEM((hp, MOBA_HEAD_DIM, seq), BF16),
                        pltpu.VMEM((hp, nqb, MOBA_HEAD_DIM), F32), pltpu.VMEM((hp, nqb, MOBA_BLOCK), F32)],
        input_output_aliases={3: 0},
        compiler_params=_params("parallel", "parallel", "arbitrary"),
        name="moba_prompt",
    )(proj, proj, proj, jnp.zeros((out_rows, MOBA_WIDTH), BF16))


MOBA_STEP_PAGES = 16


def _moba_step_kernel(pt_ref, q_ref, kn_ref, vn_ref, *refs):
    npg = MOBA_STEP_PAGES
    k_refs, cache_v = refs[:npg], refs[npg]
    o_ref, p_buf, m_buf, l_buf, g_buf, own_buf, sel_buf, v_buf, sel_smem, sem = refs[npg + 1:]
    b, j = pl.program_id(0), pl.program_id(1)
    n_seq, n_steps = pl.num_programs(0) - 1, pl.num_programs(1)
    hn, hd = MOBA_HEADS, MOBA_HEAD_DIM
    page = k_refs[0].shape[1]
    n_blocks = m_buf.shape[1]
    n_pages = 2 * n_blocks
    scale = hd ** -0.5
    lanes = page * hn
    cur = b % 2
    prev = 1 - cur

    def v_copy(par, seq, s, h, pg):
        blk = sel_smem[par, s * hn + h]
        page_id = pt_ref[seq * n_pages + 2 * blk + pg]
        return pltpu.make_async_copy(cache_v.at[page_id, :, h, :],
                                     v_buf.at[par, s, pl.ds(pg * page, page), h, :], sem.at[par])

    slots = [(s, h, pg) for s in range(MOBA_TOPK) for h in range(hn) for pg in range(2)]

    @pl.when((b >= 1) & (j == n_steps - 1))
    def _finish():
        for s, h, pg in slots:
            v_copy(prev, b - 1, s, h, pg).wait()
        s_own = own_buf[prev]
        m_sel, l_sel, pv = [], [], []
        for s in range(MOBA_TOPK):
            sel = sel_buf[prev, s]
            picks = [sel == float(n) for n in range(n_blocks)]
            m_sel.append(sum(jnp.where(pk, m_buf[prev, n], 0.0) for n, pk in enumerate(picks)))
            l_sel.append(sum(jnp.where(pk, l_buf[prev, n], 0.0) for n, pk in enumerate(picks)))
            p_s = jnp.zeros((hn, 2 * lanes), BF16)
            for n, pk in enumerate(picks):
                p_s = jnp.where(pk, p_buf[prev, n], p_s)
            pv.append(_dot(p_s, v_buf[prev, s].reshape(2 * lanes, hd).astype(BF16)))
        m_all = s_own
        for m_s in m_sel:
            m_all = jnp.maximum(m_all, m_s)
        w_own = jnp.exp(s_own - m_all)
        den = w_own
        num = w_own * vn_ref[0]
        for m_s, l_s, pv_s in zip(m_sel, l_sel, pv):
            w = jnp.exp(m_s - m_all)
            den += w * l_s
            num += w * pv_s
        o_ref[0] = (num / den).astype(o_ref.dtype)

    @pl.when(b < n_seq)
    def _score():
        q = q_ref[0]
        qb = q.astype(BF16)
        same_head = ((lax.broadcasted_iota(jnp.int32, (hn, lanes), 1) % hn)
                     == lax.broadcasted_iota(jnp.int32, (hn, lanes), 0))
        blocks = range(npg // 2)
        s_parts = [_dot_nt(qb, k_ref[0].reshape(lanes, hd).astype(BF16)) for k_ref in k_refs]
        gsum = [jnp.sum(jnp.where(same_head, s, 0.0), axis=-1, keepdims=True) for s in s_parts]
        s_m = [jnp.where(same_head, s * scale, NEG_INF) for s in s_parts]
        s_max = [jnp.max(s, axis=-1, keepdims=True) for s in s_m]
        m = [jnp.maximum(s_max[2 * i], s_max[2 * i + 1]) for i in blocks]
        p = [jnp.exp(s - m[i // 2]) for i, s in enumerate(s_m)]
        p_sum = [jnp.sum(p_i, axis=-1, keepdims=True) for p_i in p]
        for i in blocks:
            n = j * (npg // 2) + i
            g_buf[cur, n] = gsum[2 * i] + gsum[2 * i + 1]
            m_buf[cur, n] = m[i]
            l_buf[cur, n] = p_sum[2 * i] + p_sum[2 * i + 1]
            p_buf[cur, n] = jnp.concatenate([p[2 * i], p[2 * i + 1]], axis=1).astype(BF16)

        @pl.when(j == n_steps - 1)
        def _select():
            own_buf[cur] = jnp.sum(q * kn_ref[0], axis=-1, keepdims=True) * scale
            gates = [g_buf[cur, n] for n in range(n_blocks)]
            ranks = []
            for n in range(n_blocks):
                rank = jnp.zeros((hn, 1), F32)
                for n2 in range(n_blocks):
                    if n2 != n:
                        beats = (gates[n2] > gates[n]) | ((gates[n2] == gates[n]) & (n2 < n))
                        rank += beats.astype(F32)
                ranks.append(rank)
            head = lax.broadcasted_iota(jnp.int32, (hn, 1), 0)
            for s in range(MOBA_TOPK):
                sel = sum(jnp.where(ranks[n] == float(s), float(n), 0.0) for n in range(n_blocks))
                sel_buf[cur, s] = sel
                for h in range(hn):
                    sel_smem[cur, s * hn + h] = jnp.sum(jnp.where(head == h, sel, 0.0)).astype(jnp.int32)
            for s, h, pg in slots:
                v_copy(cur, b, s, h, pg).start()


def moba_step(q, k_new, v_new, cache_k, cache_v, page_table):
    nb, n_pages = page_table.shape
    page = cache_k.shape[1]
    npg = MOBA_STEP_PAGES
    n_blocks = n_pages // 2
    assert MOBA_BLOCK == 2 * page and n_pages % npg == 0 and n_blocks >= MOBA_TOPK
    hn, hd = MOBA_HEADS, MOBA_HEAD_DIM
    scored = pl.BlockSpec((1, hn, hd), lambda b, j, pt: (jnp.minimum(b, nb - 1), 0, 0))
    merged = pl.BlockSpec((1, hn, hd), lambda b, j, pt: (jnp.maximum(b - 1, 0), 0, 0))

    def page_spec(i):
        return pl.BlockSpec((1, page, hn, hd),
                            lambda b, j, pt: (pt[jnp.minimum(b, nb - 1) * n_pages + j * npg + i], 0, 0, 0))

    return pl.pallas_call(
        _moba_step_kernel,
        out_shape=jax.ShapeDtypeStruct((nb, hn, hd), BF16),
        grid_spec=pltpu.PrefetchScalarGridSpec(
            num_scalar_prefetch=1,
            grid=(nb + 1, n_pages // npg),
            in_specs=[scored, scored, merged] + [page_spec(i) for i in range(npg)]
                     + [pl.BlockSpec(memory_space=pl.ANY)],
            out_specs=merged,
            scratch_shapes=[
                pltpu.VMEM((2, n_blocks, hn, 2 * page * hn), BF16),
                pltpu.VMEM((2, n_blocks, hn, 1), F32),
                pltpu.VMEM((2, n_blocks, hn, 1), F32),
                pltpu.VMEM((2, n_blocks, hn, 1), F32),
                pltpu.VMEM((2, hn, 1), F32),
                pltpu.VMEM((2, MOBA_TOPK, hn, 1), F32),
                pltpu.VMEM((2, MOBA_TOPK, 2 * page, hn, hd), F32),
                pltpu.SMEM((2, MOBA_TOPK * hn), jnp.int32),
                pltpu.SemaphoreType.DMA((2,)),
            ],
        ),
        compiler_params=_params("arbitrary", "arbitrary"),
        name="moba_step",
    )(page_table.reshape(-1), q, k_new, v_new, *([cache_k] * npg), cache_v)


def _row_tile(m, target):
    best = None
    for t in range(16, target + 1, 16):
        if m % t == 0:
            best = t
    assert best is not None
    return best


def kernel(x_prompt, x_sample, cache_k, cache_v, state_gla, page_table, norm_mix_pre, norm_mix_post, w_in,
           w_gla_gate_up, b_gla_gate, gla_norm, w_merge, w_branch_gla, w_branch_moba, w_out, norm_ffn_pre,
           norm_ffn_post, w_ffn_gate, w_ffn_up, w_ffn_down):
    batch, seq, d = x_prompt.shape
    nb_s = x_sample.shape[0]
    mp = batch * seq
    m_all = mp + nb_s
    gla_w = 2 * GLA_QK_WIDTH + 2 * GLA_V_WIDTH
    moba0 = gla_w + GLA_GATE_RANK
    hn, hd = MOBA_HEADS, MOBA_HEAD_DIM

    w_in_t = w_in.T
    w_down = w_ffn_down.astype(BF16)
    w_up_pad = jnp.pad(w_gla_gate_up, ((0, LANE - GLA_GATE_RANK), (0, 0)))
    bias = b_gla_gate.reshape(1, -1)
    gnorm = gla_norm.reshape(1, -1)

    xp = x_prompt.reshape(mp, d)
    xs = x_sample.reshape(nb_s, d)
    tm = _row_tile(m_all, 1040)
    xn = rmsnorm_groups(xp, xs, norm_mix_pre)
    p_gla = matmul_nt(xn, w_in_t, tm, 512, 0, gla_w)
    p_low = matmul_nt(xn, w_in_t, tm, LANE, gla_w, LANE)
    p_moba = matmul_nt(xn, w_in_t, tm, 512, moba0, 3 * MOBA_WIDTH)

    o_g, state_p = gla_prompt(p_gla, p_low, w_up_pad, bias, gnorm, batch, seq, m_all)
    o_m = moba_prompt(p_moba, batch, seq, m_all)

    s_gla, s_moba = p_gla[mp:], p_moba[mp:]
    log_a = log_decay(p_low[mp:], w_up_pad, bias)

    def cols(t):
        return t.reshape(nb_s, GLA_HEADS, GLA_DK).transpose(0, 2, 1)

    o_gs, state_s = gla_step(
        cols(s_gla[:, :GLA_QK_WIDTH]), cols(s_gla[:, GLA_QK_WIDTH:2 * GLA_QK_WIDTH]), cols(log_a),
        s_gla[:, 2 * GLA_QK_WIDTH:2 * GLA_QK_WIDTH + GLA_V_WIDTH].reshape(nb_s, GLA_HEADS, GLA_DV),
        s_gla[:, 2 * GLA_QK_WIDTH + GLA_V_WIDTH:].reshape(nb_s, GLA_HEADS, GLA_DV),
        state_gla, gnorm, 4)
    q_s = s_moba[:, :MOBA_WIDTH].reshape(nb_s, hn, hd)
    k_s = s_moba[:, MOBA_WIDTH:2 * MOBA_WIDTH].reshape(nb_s, hn, hd)
    v_s = s_moba[:, 2 * MOBA_WIDTH:].reshape(nb_s, hn, hd)
    o_ms = moba_step(q_s, k_s, v_s, cache_k, cache_v, page_table)
    o_g = lax.dynamic_update_slice(o_g, o_gs.reshape(nb_s, GLA_V_WIDTH), (mp, 0))
    o_m = lax.dynamic_update_slice(o_m, o_ms.reshape(nb_s, MOBA_WIDTH), (mp, 0))

    mixed = gated_mix(xn, o_g, o_m, w_merge, w_branch_gla, w_branch_moba, tm, 256)
    h, h_n = norm_residual_groups(matmul(mixed, w_out, tm, 512), xp, xs, norm_mix_post, norm_ffn_pre)
    a = swiglu_up(h_n, w_ffn_gate, w_ffn_up, tm, 256)
    f = matmul(a, w_down, _row_tile(m_all, 520), 512)
    y_p = norm_residual_rows(f, h, norm_ffn_post, 0, mp)
    y_s = norm_residual_rows(f, h, norm_ffn_post, mp, nb_s)

    k_p = p_moba[:mp, MOBA_WIDTH:2 * MOBA_WIDTH].reshape(batch, seq, hn, hd)
    v_p = p_moba[:mp, 2 * MOBA_WIDTH:].reshape(batch, seq, hn, hd)
    return (y_p.reshape(batch, seq, d), y_s.reshape(nb_s, 1, d), k_p, v_p, state_p,
            k_s.reshape(nb_s, 1, hn, hd), v_s.reshape(nb_s, 1, hn, hd), state_s)
```

```python
import math

import jax
import jax.numpy as jnp
from jax import lax
from jax.experimental import pallas as pl
from jax.experimental.pallas import tpu as pltpu

F32 = jnp.float32
BF16 = jnp.bfloat16

GLA_HEADS = 8
GLA_DK = 128
GLA_DV = 256
GLA_QK_WIDTH = GLA_HEADS * GLA_DK
GLA_V_WIDTH = GLA_HEADS * GLA_DV
GLA_GATE_RANK = 16
GLA_GATE_TAU = 16.0
GLA_CHUNK = 64
GLA_SUB = 16
MOBA_HEADS = 16
MOBA_HEAD_DIM = 128
MOBA_WIDTH = MOBA_HEADS * MOBA_HEAD_DIM
MOBA_BLOCK = 256
MOBA_TOPK = 3
RMS_EPS = 1e-6
LANE = 128
VMEM_LIMIT = 56 * 1024 * 1024
NEG_INF = float("-inf")
LOG2_E = math.log2(math.e)
ROW_TILE = 128


def _params(*sem):
    return pltpu.CompilerParams(dimension_semantics=sem, vmem_limit_bytes=VMEM_LIMIT)


def _sigmoid(x):
    return 1.0 / (1.0 + jnp.exp(-x))


def _log_sigmoid(x):
    return jnp.minimum(x, 0.0) - jnp.log1p(jnp.exp(-jnp.abs(x)))


def _dot(a, b, precision=None):
    return jnp.dot(a, b, precision=precision, preferred_element_type=F32)


def _dot_nt(a, b):
    return lax.dot_general(a, b, (((1,), (1,)), ((), ())), preferred_element_type=F32)


def _dot_tn(a, b):
    return lax.dot_general(a, b, (((0,), (0,)), ((), ())), preferred_element_type=F32)


def _bf16(w):
    return w if w.dtype == BF16 else w.astype(BF16)


def _rms(x, w):
    return x * lax.rsqrt(jnp.mean(x * x, axis=-1, keepdims=True) + RMS_EPS) * w


def _group_specs(mp, ms, d):
    npb = mp // ROW_TILE
    assert mp % ROW_TILE == 0 and ms % ROW_TILE == 0
    p_spec = pl.BlockSpec((ROW_TILE, d), lambda i: (jnp.minimum(i, npb - 1), 0))
    s_spec = pl.BlockSpec((ROW_TILE, d), lambda i: (jnp.maximum(i - npb, 0), 0))
    return npb, p_spec, s_spec


def rmsnorm_groups(x_p, x_s, w):
    (mp, d), ms = x_p.shape, x_s.shape[0]
    npb, p_spec, s_spec = _group_specs(mp, ms, d)

    def kern(xp_ref, xs_ref, w_ref, o_ref):
        i = pl.program_id(0)

        @pl.when(i < npb)
        def _():
            o_ref[...] = _rms(xp_ref[...], w_ref[...]).astype(o_ref.dtype)

        @pl.when(i >= npb)
        def _():
            o_ref[...] = _rms(xs_ref[...], w_ref[...]).astype(o_ref.dtype)

    return pl.pallas_call(
        kern,
        out_shape=jax.ShapeDtypeStruct((mp + ms, d), BF16),
        grid=((mp + ms) // ROW_TILE,),
        in_specs=[p_spec, s_spec, pl.BlockSpec((1, d), lambda i: (0, 0))],
        out_specs=pl.BlockSpec((ROW_TILE, d), lambda i: (i, 0)),
        compiler_params=_params("arbitrary"),
        name="rmsnorm_groups",
    )(x_p, x_s, w.reshape(1, d))


def norm_residual_groups(f, x_p, x_s, w, w_next):
    (mp, d), ms = x_p.shape, x_s.shape[0]
    npb, p_spec, s_spec = _group_specs(mp, ms, d)
    row = pl.BlockSpec((ROW_TILE, d), lambda i: (i, 0))
    vec = pl.BlockSpec((1, d), lambda i: (0, 0))

    def kern(f_ref, xp_ref, xs_ref, w_ref, wn_ref, h_ref, hn_ref):
        i = pl.program_id(0)

        def body(x_ref):
            h = x_ref[...] + _rms(f_ref[...], w_ref[...])
            h_ref[...] = h
            hn_ref[...] = _rms(h, wn_ref[...]).astype(hn_ref.dtype)

        pl.when(i < npb)(lambda: body(xp_ref))
        pl.when(i >= npb)(lambda: body(xs_ref))

    return pl.pallas_call(
        kern,
        out_shape=(jax.ShapeDtypeStruct((mp + ms, d), F32), jax.ShapeDtypeStruct((mp + ms, d), BF16)),
        grid=((mp + ms) // ROW_TILE,),
        in_specs=[row, p_spec, s_spec, vec, vec],
        out_specs=(row, row),
        compiler_params=_params("arbitrary"),
        name="norm_residual_groups",
    )(f, x_p, x_s, w.reshape(1, d), w_next.reshape(1, d))


def norm_residual_rows(f, h, w, row0, rows):
    d = h.shape[1]
    tile = math.gcd(math.gcd(row0, rows), 2 * ROW_TILE)
    b0 = row0 // tile
    assert tile % 8 == 0

    def kern(f_ref, h_ref, w_ref, o_ref):
        o_ref[...] = h_ref[...] + _rms(f_ref[...], w_ref[...])

    src = pl.BlockSpec((tile, d), lambda i: (i + b0, 0))
    return pl.pallas_call(
        kern,
        out_shape=jax.ShapeDtypeStruct((rows, d), F32),
        grid=(rows // tile,),
        in_specs=[src, src, pl.BlockSpec((1, d), lambda i: (0, 0))],
        out_specs=pl.BlockSpec((tile, d), lambda i: (i, 0)),
        compiler_params=_params("arbitrary"),
        name="norm_residual_rows",
    )(f, h, w.reshape(1, d))


def _mm_kernel(x_ref, w_ref, o_ref):
    o_ref[...] = _dot(x_ref[...], _bf16(w_ref[...])).astype(o_ref.dtype)


def matmul(x, w, tm, tn, n_cols=None, col0=0, out_dtype=F32):
    m, k = x.shape
    n = w.shape[1] if n_cols is None else n_cols
    assert m % tm == 0
    return pl.pallas_call(
        _mm_kernel,
        out_shape=jax.ShapeDtypeStruct((m, n), out_dtype),
        grid=(m // tm, pl.cdiv(n, tn)),
        in_specs=[pl.BlockSpec((tm, k), lambda i, j: (i, 0)), pl.BlockSpec((k, tn), lambda i, j: (0, j + col0))],
        out_specs=pl.BlockSpec((tm, tn), lambda i, j: (i, j)),
        compiler_params=_params("parallel", "arbitrary"),
        name="matmul",
    )(x, w)


def _mm_nt_kernel(x_ref, w_ref, o_ref):
    o_ref[...] = _dot_nt(x_ref[...], _bf16(w_ref[...])).astype(o_ref.dtype)


def matmul_nt(x, w_t, tm, tn, row0, n_cols):
    m, k = x.shape
    assert m % tm == 0 and n_cols % tn == 0 and row0 % 8 == 0 and row0 + n_cols <= w_t.shape[0]
    return pl.pallas_call(
        _mm_nt_kernel,
        out_shape=jax.ShapeDtypeStruct((m, n_cols), F32),
        grid=(m // tm, n_cols // tn),
        in_specs=[pl.BlockSpec((tm, k), lambda i, j: (i, 0)),
                  pl.BlockSpec((pl.Element(tn), pl.Element(k)), lambda i, j: ((row0 // 8 + j * (tn // 8)) * 8, 0))],
        out_specs=pl.BlockSpec((tm, tn), lambda i, j: (i, j)),
        compiler_params=_params("parallel", "arbitrary"),
        name="matmul_nt",
    )(x, w_t)


def _mix_kernel(xn_ref, og_ref, om_ref, wga_ref, wgb_ref, wba_ref, wbb_ref, o_ref):
    xn = xn_ref[...]
    g_a = _sigmoid(_dot(xn, _bf16(wga_ref[...])))
    g_b = _sigmoid(_dot(xn, _bf16(wgb_ref[...])))
    mixed = g_a * _dot(og_ref[...], _bf16(wba_ref[...])) + g_b * _dot(om_ref[...], _bf16(wbb_ref[...]))
    o_ref[...] = mixed.astype(o_ref.dtype)


def gated_mix(xn, o_g, o_m, w_merge, w_bg, w_bm, tm, tn):
    m, d = xn.shape
    nb = d // tn
    assert m % tm == 0 and d % tn == 0

    def resident(width):
        return pl.BlockSpec((tm, width), lambda i, j: (i, 0), pipeline_mode=pl.Buffered(1))

    return pl.pallas_call(
        _mix_kernel,
        out_shape=jax.ShapeDtypeStruct((m, d), BF16),
        grid=(m // tm, nb),
        in_specs=[
            resident(d), resident(o_g.shape[1]), resident(o_m.shape[1]),
            pl.BlockSpec((d, tn), lambda i, j: (0, j)),
            pl.BlockSpec((d, tn), lambda i, j: (0, j + nb)),
            pl.BlockSpec((w_bg.shape[0], tn), lambda i, j: (0, j)),
            pl.BlockSpec((w_bm.shape[0], tn), lambda i, j: (0, j)),
        ],
        out_specs=pl.BlockSpec((tm, tn), lambda i, j: (i, j)),
        compiler_params=_params("parallel", "arbitrary"),
        name="gated_mix",
    )(xn, o_g, o_m, w_merge, w_merge, w_bg, w_bm)


def _swiglu_kernel(x_ref, wg_ref, wu_ref, o_ref):
    x = x_ref[...]
    g = _dot(x, _bf16(wg_ref[...]))
    u = _dot(x, _bf16(wu_ref[...]))
    o_ref[...] = (g * _sigmoid(g) * u).astype(o_ref.dtype)


def swiglu_up(x, w_gate, w_up, tm, tn):
    m, k = x.shape
    n = w_gate.shape[1]
    assert m % tm == 0 and n % tn == 0
    return pl.pallas_call(
        _swiglu_kernel,
        out_shape=jax.ShapeDtypeStruct((m, n), BF16),
        grid=(m // tm, n // tn),
        in_specs=[pl.BlockSpec((tm, k), lambda i, j: (i, 0)),
                  pl.BlockSpec((k, tn), lambda i, j: (0, j)),
                  pl.BlockSpec((k, tn), lambda i, j: (0, j))],
        out_specs=pl.BlockSpec((tm, tn), lambda i, j: (i, j)),
        compiler_params=_params("parallel", "arbitrary"),
        name="swiglu_up",
    )(x, w_gate, w_up)


def _gla_head(q, k, v, b, s_t):
    c = q.shape[0]
    nsub = c // GLA_SUB
    row = lax.broadcasted_iota(jnp.int32, (c, c), 0)
    col = lax.broadcasted_iota(jnp.int32, (c, c), 1)
    diff = row - col
    row_blk = row // GLA_SUB
    same_sub = (row_blk == col // GLA_SUB) & (col <= row)

    o_inter = _dot_nt((q * jnp.exp(b)).astype(BF16), s_t.astype(BF16))

    grp = 8
    assert GLA_SUB == 2 * grp and c % GLA_SUB == 0
    dk = q.shape[1]
    tiles = lambda x: x.reshape(c // grp, grp, dk)
    rot = lambda x, r: x if r == 0 else pltpu.roll(x, r, 1)
    q3, k3, b3 = tiles(q), tiles(k), tiles(b)
    a_same = jnp.zeros((c, c), F32)
    for r in range(grp):
        w = q3 * rot(k3, r) * jnp.exp(jnp.minimum(b3 - rot(b3, r), 0.0))
        a_same = jnp.where(diff == r, jnp.sum(w, axis=-1, keepdims=True).reshape(c, 1), a_same)
    halves = lambda x: x.reshape(nsub, 2, grp, dk)
    q_hi, k_lo, b_hi, b_lo = halves(q)[:, 1], halves(k)[:, 0], halves(b)[:, 1], halves(b)[:, 0]
    a_cross = jnp.zeros((c, c), F32)
    for r in range(grp):
        w = q_hi * rot(k_lo, r) * jnp.exp(b_hi - rot(b_lo, r))
        val = jnp.sum(w, axis=-1, keepdims=True)
        val = jnp.broadcast_to(val[:, None], (nsub, 2, grp, 1)).reshape(c, 1)
        a_cross = jnp.where((diff & (grp - 1)) == r, val, a_cross)
    same_grp = (row // grp == col // grp) & (col <= row)
    a_band = jnp.where(same_grp, a_same, a_cross)

    if nsub > 1:
        t_row = lax.broadcasted_iota(jnp.int32, (c, 1), 0)
        t_blk = t_row // GLA_SUB
        q_parts, k_parts = [], []
        for i in range(1, nsub):
            r_i = b[i * GLA_SUB - 1:i * GLA_SUB, :]
            q_i = jnp.where(t_blk == i, q * jnp.exp(jnp.minimum(b - r_i, 0.0)), 0.0)
            k_i = jnp.where(t_row < i * GLA_SUB, k * jnp.exp(jnp.minimum(r_i - b, 0.0)), 0.0)
            q_parts.append(q_i.astype(BF16))
            k_parts.append(k_i.astype(BF16))
        a_off = _dot_nt(jnp.concatenate(q_parts, axis=1), jnp.concatenate(k_parts, axis=1))
        a = jnp.where(same_sub, a_band, jnp.where(col < row_blk * GLA_SUB, a_off, 0.0))
    else:
        a = jnp.where(same_sub, a_band, 0.0)

    o = o_inter + _dot(a.astype(BF16), v.astype(BF16))
    b_last = b[c - 1:c, :]
    k_dec = k * jnp.exp(b_last - b)
    s_new = s_t * jnp.exp(b_last) + _dot_tn(v.astype(BF16), k_dec.astype(BF16))
    return o, s_new


def _gla_out(o, r, gnorm):
    return _rms(o, gnorm) * (r * _sigmoid(r))


def _log_decay(alow, wup, bias):
    z = _dot(alow.astype(BF16), wup.astype(BF16)) + bias
    return _log_sigmoid(z) / GLA_GATE_TAU


def _gla_prompt_kernel(q_ref, k_ref, v_ref, r_ref, alow_ref, wup_ref, bias_ref, gnorm_ref, o_init_ref,
                       o_ref, state_ref, st_ref):
    del o_init_ref
    t = pl.program_id(1)
    c = q_ref.shape[0]

    @pl.when(t == 0)
    def _():
        st_ref[...] = jnp.zeros_like(st_ref)

    log_a = _log_decay(alow_ref[...], wup_ref[...], bias_ref[...])
    tril = (lax.broadcasted_iota(jnp.int32, (c, c), 1) <= lax.broadcasted_iota(jnp.int32, (c, c), 0)).astype(F32)
    b_all = _dot(tril, log_a, precision=lax.Precision.HIGHEST)
    gnorm = gnorm_ref[...]
    scale = GLA_DK ** -0.5
    for h in range(GLA_HEADS):
        ks = slice(h * GLA_DK, (h + 1) * GLA_DK)
        vs = slice(h * GLA_DV, (h + 1) * GLA_DV)
        o, s_new = _gla_head(q_ref[:, ks] * scale, k_ref[:, ks], v_ref[:, vs], b_all[:, ks], st_ref[h])
        st_ref[h] = s_new
        o_ref[:, vs] = _gla_out(o, r_ref[:, vs], gnorm).astype(o_ref.dtype)

    @pl.when(t == pl.num_programs(1) - 1)
    def _():
        for h in range(GLA_HEADS):
            state_ref[0, h] = st_ref[h].T


def gla_prompt(proj, alow, w_up, bias, gnorm, batch, seq, out_rows):
    n_chunks = seq // GLA_CHUNK
    c = GLA_CHUNK

    def rows(b, t):
        return b * n_chunks + t

    return pl.pallas_call(
        _gla_prompt_kernel,
        out_shape=(jax.ShapeDtypeStruct((out_rows, GLA_V_WIDTH), BF16),
                   jax.ShapeDtypeStruct((batch, GLA_HEADS, GLA_DK, GLA_DV), F32)),
        grid=(batch, n_chunks),
        in_specs=[
            pl.BlockSpec((c, GLA_QK_WIDTH), lambda b, t: (rows(b, t), 0)),
            pl.BlockSpec((c, GLA_QK_WIDTH), lambda b, t: (rows(b, t), 1)),
            pl.BlockSpec((c, GLA_V_WIDTH), lambda b, t: (rows(b, t), 1)),
            pl.BlockSpec((c, GLA_V_WIDTH), lambda b, t: (rows(b, t), 2)),
            pl.BlockSpec((c, LANE), lambda b, t: (rows(b, t), 0)),
            pl.BlockSpec((LANE, GLA_QK_WIDTH), lambda b, t: (0, 0)),
            pl.BlockSpec((1, GLA_QK_WIDTH), lambda b, t: (0, 0)),
            pl.BlockSpec((1, GLA_DV), lambda b, t: (0, 0)),
            pl.BlockSpec(memory_space=pl.ANY),
        ],
        out_specs=(pl.BlockSpec((c, GLA_V_WIDTH), lambda b, t: (rows(b, t), 0)),
                   pl.BlockSpec((1, GLA_HEADS, GLA_DK, GLA_DV), lambda b, t: (b, 0, 0, 0))),
        scratch_shapes=[pltpu.VMEM((GLA_HEADS, GLA_DV, GLA_DK), F32)],
        input_output_aliases={8: 0},
        compiler_params=_params("parallel", "arbitrary"),
        name="gla_prompt",
    )(proj, proj, proj, proj, alow, w_up, bias, gnorm, jnp.zeros((out_rows, GLA_V_WIDTH), BF16))


def _gla_step_kernel(qt_ref, kt_ref, at_ref, v_ref, r_ref, s_ref, gnorm_ref, o_ref, snew_ref):
    bb = s_ref.shape[0]
    scale = GLA_DK ** -0.5
    gnorm = gnorm_ref[...]
    for i in range(bb):
        q_t = qt_ref[i] * scale
        k_t = kt_ref[i]
        a_t = jnp.exp(at_ref[i])
        v = v_ref[i]
        r = r_ref[i]
        outs = []
        for h in range(GLA_HEADS):
            s_new = s_ref[i, h] * a_t[:, h:h + 1] + k_t[:, h:h + 1] * v[h:h + 1, :]
            snew_ref[i, h] = s_new
            outs.append(jnp.sum(q_t[:, h:h + 1] * s_new, axis=0, keepdims=True))
        o = jnp.concatenate(outs, axis=0)
        o_ref[i] = _gla_out(o, r, gnorm).astype(o_ref.dtype)


def gla_step(q_t, k_t, a_t, v, r, state, gnorm, bb):
    nb = state.shape[0]
    col = pl.BlockSpec((bb, GLA_DK, GLA_HEADS), lambda i: (i, 0, 0))
    val = pl.BlockSpec((bb, GLA_HEADS, GLA_DV), lambda i: (i, 0, 0))
    st = pl.BlockSpec((bb, GLA_HEADS, GLA_DK, GLA_DV), lambda i: (i, 0, 0, 0))
    return pl.pallas_call(
        _gla_step_kernel,
        out_shape=(jax.ShapeDtypeStruct((nb, GLA_HEADS, GLA_DV), BF16),
                   jax.ShapeDtypeStruct(state.shape, F32)),
        grid=(nb // bb,),
        in_specs=[col, col, col, val, val, st, pl.BlockSpec((1, GLA_DV), lambda i: (0, 0))],
        out_specs=(val, st),
        compiler_params=_params("parallel"),
        name="gla_step",
    )(q_t, k_t, a_t, v, r, state, gnorm)


def _log_decay_kernel(alow_ref, wup_ref, bias_ref, o_ref):
    o_ref[...] = _log_decay(alow_ref[...], wup_ref[...], bias_ref[...])


def log_decay(alow, w_up, bias):
    m = alow.shape[0]
    return pl.pallas_call(
        _log_decay_kernel,
        out_shape=jax.ShapeDtypeStruct((m, GLA_QK_WIDTH), F32),
        name="log_decay",
    )(alow, w_up, bias)


def _masked_softmax_steps(s_lists, mask_lists, carries, v_ts):
    c_exp = (MOBA_HEAD_DIM ** -0.5) * LOG2_E
    heads = range(len(s_lists))
    s_lists = [[jnp.where(mk, s, NEG_INF) for s, mk in zip(s_lists[e], mask_lists[e])] for e in heads]
    m_news = []
    for e in heads:
        m_new = carries[e][0]
        for s in s_lists[e]:
            m_new = jnp.maximum(m_new, jnp.max(s, axis=0, keepdims=True))
        m_news.append(m_new)
    p_alls, l_news, alphas = [], [], []
    for e in heads:
        m, l, _ = carries[e]
        p_list = [jnp.exp2((s - m_news[e]) * c_exp) for s in s_lists[e]]
        alpha = jnp.exp2((m - m_news[e]) * c_exp)
        l_new = alpha * l
        for p in p_list:
            l_new = l_new + jnp.sum(p, axis=0, keepdims=True)
        p_alls.append((p_list[0] if len(p_list) == 1 else jnp.concatenate(p_list, axis=0)).astype(BF16))
        l_news.append(l_new)
        alphas.append(alpha)
    pv = [_dot(v_ts[e], p_alls[e]) for e in heads]
    return tuple((m_news[e], l_news[e], alphas[e] * carries[e][2] + pv[e]) for e in heads)


MOBA_HEADS_PER_STEP = 4
MOBA_KV_BLOCKS_PER_ITER = 4


def _moba_prompt_kernel(q_ref, k_ref, v_ref, o_init_ref, o_ref, kb_ref, vt_ref, kmean_ref, sel_ref):
    del o_init_ref
    c = pl.program_id(2)
    blk = MOBA_BLOCK
    hd = MOBA_HEAD_DIM
    nb = k_ref.shape[0] // blk
    heads = range(MOBA_HEADS_PER_STEP)

    @pl.when(c == 0)
    def _():
        for e in heads:
            ls = slice(e * hd, (e + 1) * hd)
            for n in range(nb):
                kn = k_ref[n * blk:(n + 1) * blk, ls]
                kmean_ref[e, n:n + 1, :] = jnp.sum(kn, axis=0, keepdims=True) * (1.0 / blk)
                kb_ref[e, n * blk:(n + 1) * blk, :] = kn.astype(BF16)
                vt_ref[e, :, n * blk:(n + 1) * blk] = v_ref[n * blk:(n + 1) * blk, ls].T.astype(BF16)

    own = pl.multiple_of(c * blk, blk)
    nq = q_ref.shape[0]
    qb_ts = []
    for e in heads:
        q_t = q_ref[:, e * hd:(e + 1) * hd].T
        qb_ts.append(q_t.astype(BF16))
        gate = _dot(kmean_ref[e], q_t, precision=lax.Precision.HIGHEST)
        blk_id = lax.broadcasted_iota(jnp.int32, gate.shape, 0)
        past = blk_id < c
        for n in range(nb):
            g_n = gate[n:n + 1, :]
            beats = ((gate > g_n) | ((gate == g_n) & (blk_id < n))) & past
            rank = jnp.sum(beats.astype(F32), axis=0, keepdims=True)
            sel_ref[e, n:n + 1, :] = jnp.where((rank < MOBA_TOPK) & (n < c), 1.0, 0.0)

    causal = lax.broadcasted_iota(jnp.int32, (blk, nq), 0) <= lax.broadcasted_iota(jnp.int32, (blk, nq), 1)
    init = (jnp.full((1, nq), NEG_INF, F32), jnp.zeros((1, nq), F32), jnp.zeros((hd, nq), F32))
    carry = _masked_softmax_steps(
        [[_dot(kb_ref[e, pl.ds(own, blk), :], qb_ts[e])] for e in heads], [[causal]] * len(heads),
        [init] * len(heads), [vt_ref[e, :, pl.ds(own, blk)] for e in heads])

    g = MOBA_KV_BLOCKS_PER_ITER

    def body(j, carry):
        start = pl.multiple_of(j * (g * blk), g * blk)
        s_g = [_dot(kb_ref[e, pl.ds(start, g * blk), :], qb_ts[e]) for e in heads]
        masks = [[sel_ref[e, pl.ds(g * j + i, 1), :] > 0.0 for i in range(g)] for e in heads]
        return _masked_softmax_steps([[s[i * blk:(i + 1) * blk] for i in range(g)] for s in s_g], masks, carry,
                                     [vt_ref[e, :, pl.ds(start, g * blk)] for e in heads])

    carry = lax.fori_loop(0, (c + g - 1) // g, body, carry)
    for e in heads:
        m, l, acc = carry[e]
        o_ref[:, e * hd:(e + 1) * hd] = (acc / l).T.astype(o_ref.dtype)


def moba_prompt(proj, batch, seq, out_rows):
    nqb = seq // MOBA_BLOCK
    hp = MOBA_HEADS_PER_STEP
    hg = MOBA_HEADS // hp
    w = hp * MOBA_HEAD_DIM
    assert nqb % MOBA_KV_BLOCKS_PER_ITER == 0
    return pl.pallas_call(
        _moba_prompt_kernel,
        out_shape=jax.ShapeDtypeStruct((out_rows, MOBA_WIDTH), BF16),
        grid=(batch, hg, nqb),
        in_specs=[
            pl.BlockSpec((MOBA_BLOCK, w), lambda b, h, c: (b * nqb + c, h)),
            pl.BlockSpec((seq, w), lambda b, h, c: (b, hg + h)),
            pl.BlockSpec((seq, w), lambda b, h, c: (b, 2 * hg + h)),
            pl.BlockSpec(memory_space=pl.ANY),
        ],
        out_specs=pl.BlockSpec((MOBA_BLOCK, w), lambda b, h, c: (b * nqb + c, h)),
        scratch_shapes=[pltpu.VMEM((hp, seq, MOBA_HEAD_DIM), BF16), pltpu.VMEM((hp, MOBA_HEAD_DIM, seq), BF16),
                        pltpu.VMEM((hp, nqb, MOBA_HEAD_DIM), F32), pltpu.VMEM((hp, nqb, MOBA_BLOCK), F32)],
        input_output_aliases={3: 0},
        compiler_params=_params("parallel", "parallel", "arbitrary"),
        name="moba_prompt",
    )(proj, proj, proj, jnp.zeros((out_rows, MOBA_WIDTH), BF16))


MOBA_STEP_PAGES = 16


def _moba_step_kernel(pt_ref, q_ref, kn_ref, vn_ref, *refs):
    npg = MOBA_STEP_PAGES
    k_refs, cache_v = refs[:npg], refs[npg]
    o_ref, p_buf, m_buf, l_buf, g_buf, own_buf, sel_buf, v_buf, sel_smem, sem = refs[npg + 1:]
    b, j = pl.program_id(0), pl.program_id(1)
    n_seq, n_steps = pl.num_programs(0) - 1, pl.num_programs(1)
    hn, hd = MOBA_HEADS, MOBA_HEAD_DIM
    page = k_refs[0].shape[1]
    n_blocks = m_buf.shape[1]
    n_pages = 2 * n_blocks
    scale = hd ** -0.5
    lanes = page * hn
    cur = b % 2
    prev = 1 - cur

    def v_copy(par, seq, s, h, pg):
        blk = sel_smem[par, s * hn + h]
        page_id = pt_ref[seq * n_pages + 2 * blk + pg]
        return pltpu.make_async_copy(cache_v.at[page_id, :, h, :],
                                     v_buf.at[par, s, pl.ds(pg * page, page), h, :], sem.at[par])

    slots = [(s, h, pg) for s in range(MOBA_TOPK) for h in range(hn) for pg in range(2)]

    @pl.when((b >= 1) & (j == n_steps - 1))
    def _finish():
        for s, h, pg in slots:
            v_copy(prev, b - 1, s, h, pg).wait()
        s_own = own_buf[prev]
        m_sel, l_sel, pv = [], [], []
        for s in range(MOBA_TOPK):
            sel = sel_buf[prev, s]
            picks = [sel == float(n) for n in range(n_blocks)]
            m_sel.append(sum(jnp.where(pk, m_buf[prev, n], 0.0) for n, pk in enumerate(picks)))
            l_sel.append(sum(jnp.where(pk, l_buf[prev, n], 0.0) for n, pk in enumerate(picks)))
            p_s = jnp.zeros((hn, 2 * lanes), BF16)
            for n, pk in enumerate(picks):
                p_s = jnp.where(pk, p_buf[prev, n], p_s)
            pv.append(_dot(p_s, v_buf[prev, s].reshape(2 * lanes, hd).astype(BF16)))
        m_all = s_own
        for m_s in m_sel:
            m_all = jnp.maximum(m_all, m_s)
        w_own = jnp.exp(s_own - m_all)
        den = w_own
        num = w_own * vn_ref[0]
        for m_s, l_s, pv_s in zip(m_sel, l_sel, pv):
            w = jnp.exp(m_s - m_all)
            den += w * l_s
            num += w * pv_s
        o_ref[0] = (num / den).astype(o_ref.dtype)

    @pl.when(b < n_seq)
    def _score():
        q = q_ref[0]
        qb = q.astype(BF16)
        same_head = ((lax.broadcasted_iota(jnp.int32, (hn, lanes), 1) % hn)
                     == lax.broadcasted_iota(jnp.int32, (hn, lanes), 0))
        blocks = range(npg // 2)
        s_parts = [_dot_nt(qb, k_ref[0].reshape(lanes, hd).astype(BF16)) for k_ref in k_refs]
        gsum = [jnp.sum(jnp.where(same_head, s, 0.0), axis=-1, keepdims=True) for s in s_parts]
        s_m = [jnp.where(same_head, s * scale, NEG_INF) for s in s_parts]
        s_max = [jnp.max(s, axis=-1, keepdims=True) for s in s_m]
        m = [jnp.maximum(s_max[2 * i], s_max[2 * i + 1]) for i in blocks]
        p = [jnp.exp(s - m[i // 2]) for i, s in enumerate(s_m)]
        p_sum = [jnp.sum(p_i, axis=-1, keepdims=True) for p_i in p]
        for i in blocks:
            n = j * (npg // 2) + i
            g_buf[cur, n] = gsum[2 * i] + gsum[2 * i + 1]
            m_buf[cur, n] = m[i]
            l_buf[cur, n] = p_sum[2 * i] + p_sum[2 * i + 1]
            p_buf[cur, n] = jnp.concatenate([p[2 * i], p[2 * i + 1]], axis=1).astype(BF16)

        @pl.when(j == n_steps - 1)
        def _select():
            own_buf[cur] = jnp.sum(q * kn_ref[0], axis=-1, keepdims=True) * scale
            gates = [g_buf[cur, n] for n in range(n_blocks)]
            ranks = []
            for n in range(n_blocks):
                rank = jnp.zeros((hn, 1), F32)
                for n2 in range(n_blocks):
                    if n2 != n:
                        beats = (gates[n2] > gates[n]) | ((gates[n2] == gates[n]) & (n2 < n))
                        rank += beats.astype(F32)
                ranks.append(rank)
            head = lax.broadcasted_iota(jnp.int32, (hn, 1), 0)
            for s in range(MOBA_TOPK):
                sel = sum(jnp.where(ranks[n] == float(s), float(n), 0.0) for n in range(n_blocks))
                sel_buf[cur, s] = sel
                for h in range(hn):
                    sel_smem[cur, s * hn + h] = jnp.sum(jnp.where(head == h, sel, 0.0)).astype(jnp.int32)
            for s, h, pg in slots:
                v_copy(cur, b, s, h, pg).start(priority=1)


def moba_step(q, k_new, v_new, cache_k, cache_v, page_table):
    nb, n_pages = page_table.shape
    page = cache_k.shape[1]
    npg = MOBA_STEP_PAGES
    n_blocks = n_pages // 2
    assert MOBA_BLOCK == 2 * page and n_pages % npg == 0 and n_blocks >= MOBA_TOPK
    hn, hd = MOBA_HEADS, MOBA_HEAD_DIM
    scored = pl.BlockSpec((1, hn, hd), lambda b, j, pt: (jnp.minimum(b, nb - 1), 0, 0))
    merged = pl.BlockSpec((1, hn, hd), lambda b, j, pt: (jnp.maximum(b - 1, 0), 0, 0))

    def page_spec(i):
        return pl.BlockSpec((1, page, hn, hd),
                            lambda b, j, pt: (pt[jnp.minimum(b, nb - 1) * n_pages + j * npg + i], 0, 0, 0))

    return pl.pallas_call(
        _moba_step_kernel,
        out_shape=jax.ShapeDtypeStruct((nb, hn, hd), BF16),
        grid_spec=pltpu.PrefetchScalarGridSpec(
            num_scalar_prefetch=1,
            grid=(nb + 1, n_pages // npg),
            in_specs=[scored, scored, merged] + [page_spec(i) for i in range(npg)]
                     + [pl.BlockSpec(memory_space=pl.ANY)],
            out_specs=merged,
            scratch_shapes=[
                pltpu.VMEM((2, n_blocks, hn, 2 * page * hn), BF16),
                pltpu.VMEM((2, n_blocks, hn, 1), F32),
                pltpu.VMEM((2, n_blocks, hn, 1), F32),
                pltpu.VMEM((2, n_blocks, hn, 1), F32),
                pltpu.VMEM((2, hn, 1), F32),
                pltpu.VMEM((2, MOBA_TOPK, hn, 1), F32),
                pltpu.VMEM((2, MOBA_TOPK, 2 * page, hn, hd), F32),
                pltpu.SMEM((2, MOBA_TOPK * hn), jnp.int32),
                pltpu.SemaphoreType.DMA((2,)),
            ],
        ),
        compiler_params=_params("arbitrary", "arbitrary"),
        name="moba_step",
    )(page_table.reshape(-1), q, k_new, v_new, *([cache_k] * npg), cache_v)


def _row_tile(m, target):
    best = None
    for t in range(16, target + 1, 16):
        if m % t == 0:
            best = t
    assert best is not None
    return best


def kernel(x_prompt, x_sample, cache_k, cache_v, state_gla, page_table, norm_mix_pre, norm_mix_post, w_in,
           w_gla_gate_up, b_gla_gate, gla_norm, w_merge, w_branch_gla, w_branch_moba, w_out, norm_ffn_pre,
           norm_ffn_post, w_ffn_gate, w_ffn_up, w_ffn_down):
    batch, seq, d = x_prompt.shape
    nb_s = x_sample.shape[0]
    mp = batch * seq
    m_all = mp + nb_s
    gla_w = 2 * GLA_QK_WIDTH + 2 * GLA_V_WIDTH
    moba0 = gla_w + GLA_GATE_RANK
    hn, hd = MOBA_HEADS, MOBA_HEAD_DIM

    w_in_t = w_in.T
    w_down = w_ffn_down.astype(BF16)
    w_up_pad = jnp.pad(w_gla_gate_up, ((0, LANE - GLA_GATE_RANK), (0, 0)))
    bias = b_gla_gate.reshape(1, -1)
    gnorm = gla_norm.reshape(1, -1)

    xp = x_prompt.reshape(mp, d)
    xs = x_sample.reshape(nb_s, d)
    tm = _row_tile(m_all, 1040)
    xn = rmsnorm_groups(xp, xs, norm_mix_pre)
    p_gla = matmul_nt(xn, w_in_t, tm, 512, 0, gla_w)
    p_low = matmul_nt(xn, w_in_t, tm, LANE, gla_w, LANE)
    p_moba = matmul_nt(xn, w_in_t, tm, 512, moba0, 3 * MOBA_WIDTH)

    o_g, state_p = gla_prompt(p_gla, p_low, w_up_pad, bias, gnorm, batch, seq, m_all)
    o_m = moba_prompt(p_moba, batch, seq, m_all)

    s_gla, s_moba = p_gla[mp:], p_moba[mp:]
    log_a = log_decay(p_low[mp:], w_up_pad, bias)

    def cols(t):
        return t.reshape(nb_s, GLA_HEADS, GLA_DK).transpose(0, 2, 1)

    o_gs, state_s = gla_step(
        cols(s_gla[:, :GLA_QK_WIDTH]), cols(s_gla[:, GLA_QK_WIDTH:2 * GLA_QK_WIDTH]), cols(log_a),
        s_gla[:, 2 * GLA_QK_WIDTH:2 * GLA_QK_WIDTH + GLA_V_WIDTH].reshape(nb_s, GLA_HEADS, GLA_DV),
        s_gla[:, 2 * GLA_QK_WIDTH + GLA_V_WIDTH:].reshape(nb_s, GLA_HEADS, GLA_DV),
        state_gla, gnorm, 4)
    q_s = s_moba[:, :MOBA_WIDTH].reshape(nb_s, hn, hd)
    k_s = s_moba[:, MOBA_WIDTH:2 * MOBA_WIDTH].reshape(nb_s, hn, hd)
    v_s = s_moba[:, 2 * MOBA_WIDTH:].reshape(nb_s, hn, hd)
    o_ms = moba_step(q_s, k_s, v_s, cache_k, cache_v, page_table)
    o_g = lax.dynamic_update_slice(o_g, o_gs.reshape(nb_s, GLA_V_WIDTH), (mp, 0))
    o_m = lax.dynamic_update_slice(o_m, o_ms.reshape(nb_s, MOBA_WIDTH), (mp, 0))

    mixed = gated_mix(xn, o_g, o_m, w_merge, w_branch_gla, w_branch_moba, tm, 256)
    h, h_n = norm_residual_groups(matmul(mixed, w_out, tm, 512), xp, xs, norm_mix_post, norm_ffn_pre)
    a = swiglu_up(h_n, w_ffn_gate, w_ffn_up, tm, 256)
    f = matmul(a, w_down, _row_tile(m_all, 520), 512)
    y_p = norm_residual_rows(f, h, norm_ffn_post, 0, mp)
    y_s = norm_residual_rows(f, h, norm_ffn_post, mp, nb_s)

    k_p = p_moba[:mp, MOBA_WIDTH:2 * MOBA_WIDTH].reshape(batch, seq, hn, hd)
    v_p = p_moba[:mp, 2 * MOBA_WIDTH:].reshape(batch, seq, hn, hd)
    return (y_p.reshape(batch, seq, d), y_s.reshape(nb_s, 1, d), k_p, v_p, state_p,
            k_s.reshape(nb_s, 1, hn, hd), v_s.reshape(nb_s, 1, hn, hd), state_s)
```
